```python
import math
import jax, jax.numpy as jnp
from jax import lax
import numpy as np

D_MODEL = 1024
BATCH = 8
SEQ = 4096
DEPTH = 2

CHUNK = 64
D_FF = 2816
N_BRANCH = 4
BRANCH_W = D_MODEL // 2
POOL_WINDOWS = (2, 4, 8, 16)
N_POOL_GROUPS = len(POOL_WINDOWS)
POOL_GROUP_W = BRANCH_W // N_POOL_GROUPS
SCONV_K = 3
CCONV_K = 31
SGU_BLOCK = 128
SGU_HEADS = 4
SGU_HEAD_W = BRANCH_W // SGU_HEADS
COLS_A = BRANCH_W
COLS_B = 3 * BRANCH_W
COLS_C = 2 * BRANCH_W
COLS_D = 2 * BRANCH_W
COLS_G = N_BRANCH * D_MODEL
IN_COLS = COLS_A + COLS_B + COLS_C + COLS_D + COLS_G
SPLITS = (COLS_A, COLS_A + COLS_B, COLS_A + COLS_B + COLS_C, COLS_A + COLS_B + COLS_C + COLS_D)
EPS = 1e-6

kernel_name = "hybrid_gated_parallel_mixers_macaron"


def rms_norm(x, g):
    x32 = x.astype(jnp.float32)
    y = x32 * lax.rsqrt(jnp.mean(x32 * x32, axis=-1, keepdims=True) + EPS)
    return (y * g.astype(jnp.float32)).astype(x.dtype)


def layer_norm(x, g, b):
    x32 = x.astype(jnp.float32)
    mu = jnp.mean(x32, axis=-1, keepdims=True)
    var = jnp.mean(jnp.square(x32 - mu), axis=-1, keepdims=True)
    y = (x32 - mu) * lax.rsqrt(var + EPS)
    return (y * g.astype(jnp.float32) + b.astype(jnp.float32)).astype(x.dtype)


def swiglu_half(x, g, w13, w2):
    h = rms_norm(x, g)
    a, b = jnp.split(h @ w13, 2, axis=-1)
    return x + 0.5 * ((jax.nn.silu(a) * b) @ w2)


def causal_dwconv(x, w):
    k, c = w.shape
    return lax.conv_general_dilated(
        x, w[:, None, :].astype(x.dtype), window_strides=(1,), padding=[(k - 1, 0)],
        dimension_numbers=("NWC", "WIO", "NWC"), feature_group_count=c)


def pool_mixer(a, pool_w, pool_scale):
    bn, s, _ = a.shape
    a32 = a.astype(jnp.float32).reshape(bn, s, N_POOL_GROUPS, POOL_GROUP_W)
    cs = jnp.cumsum(a32, axis=1)
    t = jnp.arange(s)
    outs = []
    for gi, win in enumerate(POOL_WINDOWS):
        c = cs[:, :, gi]
        lag = jnp.pad(c, ((0, 0), (win, 0), (0, 0)))[:, :s]
        cnt = jnp.minimum(t + 1, win).astype(jnp.float32)[None, :, None]
        outs.append((c - lag) / cnt - a32[:, :, gi])
    d = jnp.stack(outs, axis=2).astype(a.dtype)
    y = jnp.einsum('bsgc,gcd->bsgd', d, pool_w).reshape(bn, s, BRANCH_W)
    return y * pool_scale


def short_conv_mixer(p, sconv_w):
    xin, bg, cg = jnp.split(p, 3, axis=-1)
    return bg * causal_dwconv(cg * xin, sconv_w)


def conformer_conv_mixer(p, cconv_w, ln_g, ln_b):
    a, b = jnp.split(p, 2, axis=-1)
    y = a * jax.nn.sigmoid(b)
    y = causal_dwconv(y, cconv_w)
    y = layer_norm(y, ln_g, ln_b)
    return jax.nn.silu(y)


def spatial_gating_mixer(p, ln_g, ln_b, sgu_w, sgu_b):
    u, v = jnp.split(jax.nn.gelu(p), 2, axis=-1)
    v = layer_norm(v, ln_g, ln_b)
    bn, s, _ = v.shape
    v = v.reshape(bn, s // SGU_BLOCK, SGU_BLOCK, SGU_HEADS, SGU_HEAD_W)
    i = jnp.arange(SGU_BLOCK)
    mask = (i[None, :] // CHUNK) <= (i[:, None] // CHUNK)
    w = jnp.where(mask[None], sgu_w, jnp.zeros((), sgu_w.dtype))
    z = jnp.einsum('hij,bnjhc->bnihc', w, v) + sgu_b.T[None, None, :, :, None]
    return u * z.reshape(bn, s, BRANCH_W)


def setup_inputs(seed: int = 0) -> dict:
    key = jax.random.key(seed)
    ks = jax.random.split(key, 24)
    f32 = jnp.float32

    def nrm(k, shape, scale):
        return jax.random.normal(k, shape, f32) * scale

    def gain(k, shape):
        return 1.0 + 0.05 * jax.random.normal(k, shape, f32)

    L = DEPTH
    return {
        "x": jax.random.normal(ks[0], (BATCH, SEQ, D_MODEL), f32),
        "ffn1_norm": gain(ks[1], (L, D_MODEL)),
        "ffn1_w13": nrm(ks[2], (L, D_MODEL, 2 * D_FF), D_MODEL ** -0.5),
        "ffn1_w2": nrm(ks[3], (L, D_FF, D_MODEL), D_FF ** -0.5),
        "mix_norm": gain(ks[4], (L, D_MODEL)),
        "w_in": nrm(ks[5], (L, D_MODEL, IN_COLS), D_MODEL ** -0.5),
        "pool_w": nrm(ks[6], (L, N_POOL_GROUPS, POOL_GROUP_W, POOL_GROUP_W), POOL_GROUP_W ** -0.5),
        "pool_scale": gain(ks[7], (L, BRANCH_W)),
        "sconv_w": nrm(ks[8], (L, SCONV_K, BRANCH_W), SCONV_K ** -0.5),
        "cconv_w": nrm(ks[9], (L, CCONV_K, BRANCH_W), CCONV_K ** -0.5),
        "cconv_ln_g": gain(ks[10], (L, BRANCH_W)),
        "cconv_ln_b": nrm(ks[11], (L, BRANCH_W), 0.02),
        "sgu_ln_g": gain(ks[12], (L, BRANCH_W)),
        "sgu_ln_b": nrm(ks[13], (L, BRANCH_W), 0.02),
        "sgu_w": nrm(ks[14], (L, SGU_HEADS, SGU_BLOCK, SGU_BLOCK), SGU_BLOCK ** -0.5),
        "sgu_b": gain(ks[15], (L, SGU_HEADS, SGU_BLOCK)),
        "w_up": nrm(ks[16], (L, N_BRANCH, BRANCH_W, D_MODEL), BRANCH_W ** -0.5),
        "w_out": nrm(ks[17], (L, D_MODEL, D_MODEL), D_MODEL ** -0.5),
        "ffn2_norm": gain(ks[18], (L, D_MODEL)),
        "ffn2_w13": nrm(ks[19], (L, D_MODEL, 2 * D_FF), D_MODEL ** -0.5),
        "ffn2_w2": nrm(ks[20], (L, D_FF, D_MODEL), D_FF ** -0.5),
        "final_norm": gain(ks[21], (D_MODEL,)),
    }


def reference(x, ffn1_norm, ffn1_w13, ffn1_w2, mix_norm, w_in, pool_w, pool_scale,
              sconv_w, cconv_w, cconv_ln_g, cconv_ln_b, sgu_ln_g, sgu_ln_b, sgu_w, sgu_b,
              w_up, w_out, ffn2_norm, ffn2_w13, ffn2_w2, final_norm):
    bn, s, d = x.shape
    for l in range(DEPTH):
        x = swiglu_half(x, ffn1_norm[l], ffn1_w13[l], ffn1_w2[l])
        h = rms_norm(x, mix_norm[l])
        proj = h @ w_in[l]
        pa, pb, pc, pd, pg = jnp.split(proj, SPLITS, axis=-1)
        ya = pool_mixer(pa, pool_w[l], pool_scale[l])
        yb = short_conv_mixer(pb, sconv_w[l])
        yc = conformer_conv_mixer(pc, cconv_w[l], cconv_ln_g[l], cconv_ln_b[l])
        yd = spatial_gating_mixer(pd, sgu_ln_g[l], sgu_ln_b[l], sgu_w[l], sgu_b[l])
        y = jnp.stack([ya, yb, yc, yd], axis=2)
        up = jnp.einsum('bsgc,gcd->bsgd', y, w_up[l])
        gates = jax.nn.sigmoid(pg.reshape(bn, s, N_BRANCH, d))
        merged = jnp.sum(gates * up, axis=2)
        x = x + merged @ w_out[l]
        x = swiglu_half(x, ffn2_norm[l], ffn2_w13[l], ffn2_w2[l])
    return rms_norm(x, final_norm)
```

```python
import functools

import jax
import jax.numpy as jnp
from jax import lax
from jax.experimental import pallas as pl
from jax.experimental.pallas import tpu as pltpu

D_MODEL = 1024
D_FF = 2816
BRANCH_W = 512
N_BRANCH = 4
POOL_WINDOWS = (2, 4, 8, 16)
POOL_GROUP_W = 128
SCONV_K = 3
CCONV_K = 31
SGU_BLOCK = 128
SGU_HEADS = 4
SGU_HEAD_W = 128
CHUNK = 64
EPS = 1e-6

OFF_A = 0
OFF_B = 512
OFF_C = 2048
OFF_D = 3072
OFF_G = 4096
IN_COLS = 8192

FFN_TM = 512
FFN_TF = 256
MIX_TM = 512
HALO_A = 16
HALO_B = 8
HALO_C = 32
CONV_ROWS = 32
VMEM_LIMIT = 56 * 1024 * 1024

BF16 = jnp.bfloat16
F32 = jnp.float32


def _dot(a, b):
    return jnp.dot(a, b, preferred_element_type=F32)


def _sigmoid(x):
    return 0.5 * jnp.tanh(0.5 * x) + 0.5


def _gelu_tanh(x):
    c = 0.7978845608028654
    return 0.5 * x * (1.0 + jnp.tanh(c * (x + 0.044715 * (x * x * x))))


def _rms(x, g):
    ms = jnp.mean(x * x, axis=-1, keepdims=True)
    return x * lax.rsqrt(ms + EPS) * g


def _layer_norm(x, g, b):
    mu = jnp.mean(x, axis=-1, keepdims=True)
    xc = x - mu
    var = jnp.mean(xc * xc, axis=-1, keepdims=True)
    return xc * lax.rsqrt(var + EPS) * g + b


def _ffn_kernel(x_ref, g_ref, w13_ref, w2_ref, *rest, final):
    if final:
        fg_ref, o_ref, h_ref, act_ref = rest
    else:
        o_ref, h_ref, act_ref = rest
    x = x_ref[...]
    h_ref[...] = _rms(x, g_ref[...]).astype(BF16)
    for c in range(D_FF // FFN_TF):
        lo = c * FFN_TF
        h = h_ref[...]
        a = _dot(h, w13_ref[:, lo:lo + FFN_TF])
        b = _dot(h, w13_ref[:, D_FF + lo:D_FF + lo + FFN_TF])
        act_ref[:, lo:lo + FFN_TF] = (a * _sigmoid(a) * b).astype(BF16)
    y = _dot(act_ref[...], w2_ref[...])
    out = x_ref[...] + 0.5 * y
    if final:
        out = _rms(out, fg_ref[...])
    o_ref[...] = out


def _const_spec(shape):
    nd = len(shape)
    return pl.BlockSpec(shape, lambda *_: (0,) * nd, pipeline_mode=pl.Buffered(1))


def _ffn(x, g, w13, w2, final_g=None):
    t = x.shape[0]
    final = final_g is not None
    in_specs = [
        pl.BlockSpec((FFN_TM, D_MODEL), lambda i: (i, 0)),
        _const_spec((1, D_MODEL)),
        _const_spec((D_MODEL, 2 * D_FF)),
        _const_spec((D_FF, D_MODEL)),
    ]
    args = [x, g.reshape(1, D_MODEL), w13, w2]
    if final:
        in_specs.append(_const_spec((1, D_MODEL)))
        args.append(final_g.reshape(1, D_MODEL))
    return pl.pallas_call(
        functools.partial(_ffn_kernel, final=final),
        grid=(t // FFN_TM,),
        in_specs=in_specs,
        out_specs=pl.BlockSpec((FFN_TM, D_MODEL), lambda i: (i, 0)),
        out_shape=jax.ShapeDtypeStruct((t, D_MODEL), F32),
        scratch_shapes=[
            pltpu.VMEM((FFN_TM, D_MODEL), BF16),
            pltpu.VMEM((FFN_TM, D_FF), BF16),
        ],
        compiler_params=pltpu.CompilerParams(
            dimension_semantics=("arbitrary",), vmem_limit_bytes=VMEM_LIMIT),
        name="ffn_final" if final else "ffn",
    )(*args)


def _mixer_kernel(x_ref, g_ref, w_in_ref, pool_w_ref, pool_scale_ref, sconv_w_ref,
                  cconv_w_ref, cln_g_ref, cln_b_ref, sln_g_ref, sln_b_ref,
                  sgu_w_ref, sgu_bias_ref, w_up_ref, w_out_ref,
                  o_ref,
                  h_ref, ext_a, ext_b, ext_c, y_ref, u_ref, v_ref, m_ref):
    tm = MIX_TM
    s = pl.program_id(1)

    @pl.when(s == 0)
    def _():
        ext_a[0:HALO_A, :] = jnp.zeros((HALO_A, BRANCH_W), F32)
        ext_b[0:HALO_B, :] = jnp.zeros((HALO_B, BRANCH_W), F32)
        ext_c[0:HALO_C, :] = jnp.zeros((HALO_C, BRANCH_W), F32)

    h_ref[...] = _rms(x_ref[...], g_ref[...]).astype(BF16)

    def proj(lo, width):
        return _dot(h_ref[...], w_in_ref[:, lo:lo + width])

    ext_a[HALO_A:HALO_A + tm, :] = proj(OFF_A, BRANCH_W)
    t_idx = s * tm + lax.broadcasted_iota(jnp.int32, (tm, 1), 0)
    for gi, win in enumerate(POOL_WINDOWS):
        c0 = gi * POOL_GROUP_W
        cols = slice(c0, c0 + POOL_GROUP_W)
        tok = ext_a[HALO_A:HALO_A + tm, cols]
        acc = tok
        for j in range(1, win):
            acc = acc + ext_a[HALO_A - j:HALO_A - j + tm, cols]
        inv_cnt = 1.0 / jnp.minimum(t_idx + 1, win).astype(F32)
        d = acc * inv_cnt - tok
        ya = _dot(d.astype(BF16), pool_w_ref[gi]) * pool_scale_ref[:, cols]
        y_ref[0, :, cols] = ya.astype(BF16)
    ext_a[0:HALO_A, :] = ext_a[tm:tm + HALO_A, :]

    ext_b[HALO_B:HALO_B + tm, :] = proj(OFF_B, BRANCH_W) * proj(OFF_B + 2 * BRANCH_W, BRANCH_W)
    conv = None
    for k in range(SCONV_K):
        off = HALO_B - (SCONV_K - 1) + k
        term = sconv_w_ref[k:k + 1, :] * ext_b[off:off + tm, :]
        conv = term if conv is None else conv + term
    y_ref[1] = (proj(OFF_B + BRANCH_W, BRANCH_W) * conv).astype(BF16)
    ext_b[0:HALO_B, :] = ext_b[tm:tm + HALO_B, :]

    glu_a = proj(OFF_C, BRANCH_W)
    glu_b = proj(OFF_C + BRANCH_W, BRANCH_W)
    ext_c[HALO_C:HALO_C + tm, :] = glu_a * _sigmoid(glu_b)
    for r in range(tm // CONV_ROWS):
        r0 = r * CONV_ROWS
        acc = None
        for k in range(CCONV_K):
            off = r0 + HALO_C - (CCONV_K - 1) + k
            term = cconv_w_ref[k:k + 1, :] * ext_c[off:off + CONV_ROWS, :]
            acc = term if acc is None else acc + term
        yc = _layer_norm(acc, cln_g_ref[...], cln_b_ref[...])
        y_ref[2, r0:r0 + CONV_ROWS, :] = (yc * _sigmoid(yc)).astype(BF16)
    ext_c[0:HALO_C, :] = ext_c[tm:tm + HALO_C, :]

    u_ref[...] = _gelu_tanh(proj(OFF_D, BRANCH_W))
    v = _gelu_tanh(proj(OFF_D + BRANCH_W, BRANCH_W))
    v_ref[...] = _layer_norm(v, sln_g_ref[...], sln_b_ref[...]).astype(BF16)
    row = lax.broadcasted_iota(jnp.int32, (SGU_BLOCK, SGU_BLOCK), 0)
    col = lax.broadcasted_iota(jnp.int32, (SGU_BLOCK, SGU_BLOCK), 1)
    mask = (col // CHUNK) <= (row // CHUNK)
    n_blk = tm // SGU_BLOCK
    for hd in range(SGU_HEADS):
        hc = slice(hd * SGU_HEAD_W, (hd + 1) * SGU_HEAD_W)
        w = jnp.where(mask, sgu_w_ref[hd], 0.0).astype(BF16)
        rhs = jnp.concatenate(
            [v_ref[n * SGU_BLOCK:(n + 1) * SGU_BLOCK, hc] for n in range(n_blk)], axis=1)
        z = _dot(w, rhs)
        for n in range(n_blk):
            rows = slice(n * SGU_BLOCK, (n + 1) * SGU_BLOCK)
            zn = z[:, n * SGU_HEAD_W:(n + 1) * SGU_HEAD_W] + sgu_bias_ref[:, hc]
            y_ref[3, rows, hc] = (u_ref[rows, hc] * zn).astype(BF16)

    mc = 256
    for j in range(D_MODEL // mc):
        merged = None
        for g in range(N_BRANCH):
            gate = _sigmoid(proj(OFF_G + g * D_MODEL + j * mc, mc))
            up = _dot(y_ref[g], w_up_ref[g, :, j * mc:(j + 1) * mc])
            merged = gate * up if merged is None else merged + gate * up
        m_ref[:, j * mc:(j + 1) * mc] = merged.astype(BF16)
    o_ref[...] = x_ref[...] + _dot(m_ref[...], w_out_ref[...])


def _mixer(x, batch, seq, g, w_in, pool_w, pool_scale, sconv_w, cconv_w, cln_g, cln_b,
           sln_g, sln_b, sgu_w, sgu_bias, w_up, w_out):
    tm = MIX_TM
    n_s = seq // tm
    row = lambda v: v.reshape(1, -1)
    args = [x, row(g), w_in, pool_w, row(pool_scale), sconv_w, cconv_w, row(cln_g), row(cln_b),
            row(sln_g), row(sln_b), sgu_w, sgu_bias, w_up, w_out]
    in_specs = [pl.BlockSpec((tm, D_MODEL), lambda b, s: (b * n_s + s, 0))]
    in_specs += [_const_spec(a.shape) for a in args[1:]]
    return pl.pallas_call(
        _mixer_kernel,
        grid=(batch, n_s),
        in_specs=in_specs,
        out_specs=pl.BlockSpec((tm, D_MODEL), lambda b, s: (b * n_s + s, 0)),
        out_shape=jax.ShapeDtypeStruct(x.shape, F32),
        scratch_shapes=[
            pltpu.VMEM((tm, D_MODEL), BF16),
            pltpu.VMEM((HALO_A + tm, BRANCH_W), F32),
            pltpu.VMEM((HALO_B + tm, BRANCH_W), F32),
            pltpu.VMEM((HALO_C + tm, BRANCH_W), F32),
            pltpu.VMEM((N_BRANCH, tm, BRANCH_W), BF16),
            pltpu.VMEM((tm, BRANCH_W), F32),
            pltpu.VMEM((tm, BRANCH_W), BF16),
            pltpu.VMEM((tm, D_MODEL), BF16),
        ],
        compiler_params=pltpu.CompilerParams(
            dimension_semantics=("arbitrary", "arbitrary"), vmem_limit_bytes=VMEM_LIMIT),
        name="mixer",
    )(*args)


def kernel(x, ffn1_norm, ffn1_w13, ffn1_w2, mix_norm, w_in, pool_w, pool_scale, sconv_w, cconv_w,
           cconv_ln_g, cconv_ln_b, sgu_ln_g, sgu_ln_b, sgu_w, sgu_b, w_up, w_out, ffn2_norm,
           ffn2_w13, ffn2_w2, final_norm):
    bn, s, d = x.shape
    depth = w_in.shape[0]
    assert d == D_MODEL and s % MIX_TM == 0 and (bn * s) % FFN_TM == 0
    xf = x.reshape(bn * s, d)
    ffn1_w13, ffn1_w2, ffn2_w13, ffn2_w2, w_in, pool_w, w_up, w_out = (
        w.astype(BF16) for w in (ffn1_w13, ffn1_w2, ffn2_w13, ffn2_w2, w_in, pool_w, w_up, w_out))
    sgu_bias = jnp.repeat(jnp.swapaxes(sgu_b, 1, 2), SGU_HEAD_W, axis=2)
    for l in range(depth):
        xf = _ffn(xf, ffn1_norm[l], ffn1_w13[l], ffn1_w2[l])
        xf = _mixer(xf, bn, s, mix_norm[l], w_in[l], pool_w[l], pool_scale[l], sconv_w[l],
                    cconv_w[l], cconv_ln_g[l], cconv_ln_b[l], sgu_ln_g[l], sgu_ln_b[l],
                    sgu_w[l], sgu_bias[l], w_up[l], w_out[l])
        last = l == depth - 1
        xf = _ffn(xf, ffn2_norm[l], ffn2_w13[l], ffn2_w2[l], final_norm if last else None)
    return xf.reshape(bn, s, d)
```

```python
import functools

import jax
import jax.numpy as jnp
from jax import lax
from jax.experimental import pallas as pl
from jax.experimental.pallas import tpu as pltpu

D_MODEL = 1024
D_FF = 2816
BRANCH_W = 512
N_BRANCH = 4
POOL_WINDOWS = (2, 4, 8, 16)
SCONV_K = 3
CCONV_K = 31
SGU_BLOCK = 128
SGU_HEADS = 4
CHUNK = 64
EPS = 1e-6

LANES = 128
N_LT = BRANCH_W // LANES

OFF_A = 0
OFF_B = 512
OFF_C = 2048
OFF_D = 3072
OFF_G = 4096

FFN_TM = 512
FFN_TF = 256
MIX_TM = 512
HALO_A = 16
HALO_B = 8
HALO_C = 32
CONV_ROWS = 32
MERGE_COLS = 256
VMEM_LIMIT = 56 * 1024 * 1024

BF16 = jnp.bfloat16
F32 = jnp.float32


def _dot(a, b):
    return jnp.dot(a, b, preferred_element_type=F32)


def _sigmoid(x):
    return 0.5 * jnp.tanh(0.5 * x) + 0.5


def _gelu_tanh(x):
    c = 0.7978845608028654
    return 0.5 * x * (1.0 + jnp.tanh(c * (x + 0.044715 * (x * x * x))))


def _rms(x, g):
    ms = jnp.mean(x * x, axis=-1, keepdims=True)
    return x * lax.rsqrt(ms + EPS) * g


def _layer_norm(x, g, b):
    mu = jnp.mean(x, axis=-1, keepdims=True)
    xc = x - mu
    var = jnp.mean(xc * xc, axis=-1, keepdims=True)
    return xc * lax.rsqrt(var + EPS) * g + b


def _layer_spec(arr, layer):
    nd = arr.ndim
    return pl.BlockSpec((None,) + arr.shape[1:], lambda *_: (layer,) + (0,) * (nd - 1),
                        pipeline_mode=pl.Buffered(1))


def _ffn_kernel(x_ref, g_ref, w13_ref, w2_ref, *rest, final):
    if final:
        fg_ref, o_ref, h_ref, act_ref = rest
    else:
        o_ref, h_ref, act_ref = rest
    x = x_ref[...]
    h_ref[...] = _rms(x, g_ref[...]).astype(BF16)
    for c in range(D_FF // FFN_TF):
        lo = c * FFN_TF
        h = h_ref[...]
        a = _dot(h, w13_ref[:, lo:lo + FFN_TF])
        b = _dot(h, w13_ref[:, D_FF + lo:D_FF + lo + FFN_TF])
        act_ref[:, lo:lo + FFN_TF] = (a * _sigmoid(a) * b).astype(BF16)
    y = _dot(act_ref[...], w2_ref[...])
    out = x_ref[...] + 0.5 * y
    if final:
        out = _rms(out, fg_ref[...])
    o_ref[...] = out


def _ffn(x, layer, g, w13, w2, final_g=None):
    t = x.shape[0]
    final = final_g is not None
    args = [x, g, w13, w2]
    in_specs = [pl.BlockSpec((FFN_TM, D_MODEL), lambda i: (i, 0))]
    in_specs += [_layer_spec(a, layer) for a in args[1:]]
    if final:
        args.append(final_g)
        in_specs.append(_layer_spec(final_g, 0))
    return pl.pallas_call(
        functools.partial(_ffn_kernel, final=final),
        grid=(t // FFN_TM,),
        in_specs=in_specs,
        out_specs=pl.BlockSpec((FFN_TM, D_MODEL), lambda i: (i, 0)),
        out_shape=jax.ShapeDtypeStruct((t, D_MODEL), F32),
        scratch_shapes=[
            pltpu.VMEM((FFN_TM, D_MODEL), BF16),
            pltpu.VMEM((FFN_TM, D_FF), BF16),
        ],
        compiler_params=pltpu.CompilerParams(
            dimension_semantics=("arbitrary",), vmem_limit_bytes=VMEM_LIMIT),
        name="ffn_final" if final else "ffn",
    )(*args)


def _mixer_kernel(x_ref, g_ref, w_in_ref, pool_w_ref, pool_scale_ref, sconv_w_ref,
                  cconv_w_ref, cln_g_ref, cln_b_ref, sln_g_ref, sln_b_ref,
                  sgu_w_ref, sgu_bias_ref, w_up_ref, w_out_ref,
                  o_ref,
                  h_ref, ext_a, ext_b, ext_c, y_ref, u_ref, v_ref, m_ref):
    tm = MIX_TM
    s = pl.program_id(1)

    @pl.when(s == 0)
    def _():
        ext_a[:, 0:HALO_A, :] = jnp.zeros((N_LT, HALO_A, LANES), F32)
        ext_b[:, 0:HALO_B, :] = jnp.zeros((N_LT, HALO_B, LANES), F32)
        ext_c[:, 0:HALO_C, :] = jnp.zeros((N_LT, HALO_C, LANES), F32)

    h_ref[...] = _rms(x_ref[...], g_ref[...]).astype(BF16)

    def proj(lo, width):
        return _dot(h_ref[...], w_in_ref[:, lo:lo + width])

    def put_tiles(ext, halo, val):
        for j in range(N_LT):
            ext[j, halo:halo + tm, :] = val[:, j * LANES:(j + 1) * LANES]

    put_tiles(ext_c, HALO_C, proj(OFF_C, BRANCH_W) * _sigmoid(proj(OFF_C + BRANCH_W, BRANCH_W)))

    def conv_chunk(r):
        r0 = r * CONV_ROWS
        tiles = []
        for j in range(N_LT):
            acc = None
            for k in range(CCONV_K):
                off = r0 + HALO_C - (CCONV_K - 1) + k
                term = cconv_w_ref[k:k + 1, j * LANES:(j + 1) * LANES] * ext_c[j, off:off + CONV_ROWS, :]
                acc = term if acc is None else acc + term
            tiles.append(acc)
        yc = _layer_norm(jnp.concatenate(tiles, axis=1), cln_g_ref[...], cln_b_ref[...])
        y_ref[2, r0:r0 + CONV_ROWS, :] = (yc * _sigmoid(yc)).astype(BF16)

    def task_pool():
        put_tiles(ext_a, HALO_A, proj(OFF_A, BRANCH_W))
        t_idx = s * tm + lax.broadcasted_iota(jnp.int32, (tm, 1), 0)
        for gi, win in enumerate(POOL_WINDOWS):
            cols = slice(gi * LANES, (gi + 1) * LANES)
            tok = ext_a[gi, HALO_A:HALO_A + tm, :]
            acc = tok
            for j in range(1, win):
                acc = acc + ext_a[gi, HALO_A - j:HALO_A - j + tm, :]
            inv_cnt = 1.0 / jnp.minimum(t_idx + 1, win).astype(F32)
            d = acc * inv_cnt - tok
            ya = _dot(d.astype(BF16), pool_w_ref[gi]) * pool_scale_ref[:, cols]
            y_ref[0, :, cols] = ya.astype(BF16)

    def task_sconv_in():
        put_tiles(ext_b, HALO_B, proj(OFF_B, BRANCH_W) * proj(OFF_B + 2 * BRANCH_W, BRANCH_W))

    def task_sconv_out():
        bg = proj(OFF_B + BRANCH_W, BRANCH_W)
        for j in range(N_LT):
            cols = slice(j * LANES, (j + 1) * LANES)
            conv = None
            for k in range(SCONV_K):
                off = HALO_B - (SCONV_K - 1) + k
                term = sconv_w_ref[k:k + 1, cols] * ext_b[j, off:off + tm, :]
                conv = term if conv is None else conv + term
            y_ref[1, :, cols] = (bg[:, cols] * conv).astype(BF16)

    def task_sgu_u():
        u_ref[...] = _gelu_tanh(proj(OFF_D, BRANCH_W))

    def task_sgu_v():
        v = _gelu_tanh(proj(OFF_D + BRANCH_W, BRANCH_W))
        v_ref[...] = _layer_norm(v, sln_g_ref[...], sln_b_ref[...]).astype(BF16)

    def task_sgu_mix():
        row = lax.broadcasted_iota(jnp.int32, (SGU_BLOCK, SGU_BLOCK), 0)
        col = lax.broadcasted_iota(jnp.int32, (SGU_BLOCK, SGU_BLOCK), 1)
        mask = (col // CHUNK) <= (row // CHUNK)
        n_blk = tm // SGU_BLOCK
        for hd in range(SGU_HEADS):
            hc = slice(hd * LANES, (hd + 1) * LANES)
            w = jnp.where(mask, sgu_w_ref[hd], 0.0).astype(BF16)
            rhs = jnp.concatenate(
                [v_ref[n * SGU_BLOCK:(n + 1) * SGU_BLOCK, hc] for n in range(n_blk)], axis=1)
            z = _dot(w, rhs)
            for n in range(n_blk):
                rows = slice(n * SGU_BLOCK, (n + 1) * SGU_BLOCK)
                zn = z[:, n * LANES:(n + 1) * LANES] + sgu_bias_ref[:, hc]
                y_ref[3, rows, hc] = (u_ref[rows, hc] * zn).astype(BF16)

    n_conv = tm // CONV_ROWS
    tasks = [task_pool, task_sconv_in, task_sconv_out, task_sgu_u, task_sgu_v]
    per_task = n_conv // len(tasks)
    r = 0
    for task in tasks:
        task()
        for _ in range(per_task):
            conv_chunk(r)
            r += 1
    task_sgu_mix()
    while r < n_conv:
        conv_chunk(r)
        r += 1

    for j in range(N_LT):
        ext_a[j, 0:HALO_A, :] = ext_a[j, tm:tm + HALO_A, :]
        ext_b[j, 0:HALO_B, :] = ext_b[j, tm:tm + HALO_B, :]
        ext_c[j, 0:HALO_C, :] = ext_c[j, tm:tm + HALO_C, :]

    for j in range(D_MODEL // MERGE_COLS):
        cols = slice(j * MERGE_COLS, (j + 1) * MERGE_COLS)
        merged = None
        for g in range(N_BRANCH):
            gate = _sigmoid(proj(OFF_G + g * D_MODEL + j * MERGE_COLS, MERGE_COLS))
            up = _dot(y_ref[g], w_up_ref[g, :, cols])
            merged = gate * up if merged is None else merged + gate * up
        m_ref[:, cols] = merged.astype(BF16)
    o_ref[...] = x_ref[...] + _dot(m_ref[...], w_out_ref[...])


def _mixer(x, batch, seq, layer, params):
    tm = MIX_TM
    n_s = seq // tm
    in_specs = [pl.BlockSpec((tm, D_MODEL), lambda b, s: (b * n_s + s, 0))]
    in_specs += [_layer_spec(a, layer) for a in params]
    return pl.pallas_call(
        _mixer_kernel,
        grid=(batch, n_s),
        in_specs=in_specs,
        out_specs=pl.BlockSpec((tm, D_MODEL), lambda b, s: (b * n_s + s, 0)),
        out_shape=jax.ShapeDtypeStruct(x.shape, F32),
        scratch_shapes=[
            pltpu.VMEM((tm, D_MODEL), BF16),
            pltpu.VMEM((N_LT, HALO_A + tm, LANES), F32),
            pltpu.VMEM((N_LT, HALO_B + tm, LANES), F32),
            pltpu.VMEM((N_LT, HALO_C + tm, LANES), F32),
            pltpu.VMEM((N_BRANCH, tm, BRANCH_W), BF16),
            pltpu.VMEM((tm, BRANCH_W), F32),
            pltpu.VMEM((tm, BRANCH_W), BF16),
            pltpu.VMEM((tm, D_MODEL), BF16),
        ],
        compiler_params=pltpu.CompilerParams(
            dimension_semantics=("arbitrary", "arbitrary"), vmem_limit_bytes=VMEM_LIMIT),
        name="mixer",
    )(x, *params)


def _prepare(ffn1_norm, ffn1_w13, ffn1_w2, mix_norm, w_in, pool_w, pool_scale, sconv_w, cconv_w,
             cconv_ln_g, cconv_ln_b, sgu_ln_g, sgu_ln_b, sgu_w, sgu_b, w_up, w_out, ffn2_norm,
             ffn2_w13, ffn2_w2, final_norm):
    row = lambda v: v.reshape(v.shape[0], 1, v.shape[-1])
    bf = lambda w: w.astype(BF16)
    sgu_bias = jnp.repeat(jnp.swapaxes(sgu_b, 1, 2), LANES, axis=2)
    ffn1 = (row(ffn1_norm), bf(ffn1_w13), bf(ffn1_w2))
    ffn2 = (row(ffn2_norm), bf(ffn2_w13), bf(ffn2_w2))
    mixer = (row(mix_norm), bf(w_in), bf(pool_w), row(pool_scale), sconv_w, cconv_w,
             row(cconv_ln_g), row(cconv_ln_b), row(sgu_ln_g), row(sgu_ln_b), sgu_w, sgu_bias,
             bf(w_up), bf(w_out))
    return ffn1, mixer, ffn2, final_norm.reshape(1, 1, -1)


def kernel(x, ffn1_norm, ffn1_w13, ffn1_w2, mix_norm, w_in, pool_w, pool_scale, sconv_w, cconv_w,
           cconv_ln_g, cconv_ln_b, sgu_ln_g, sgu_ln_b, sgu_w, sgu_b, w_up, w_out, ffn2_norm,
           ffn2_w13, ffn2_w2, final_norm):
    bn, s, d = x.shape
    depth = w_in.shape[0]
    assert d == D_MODEL and s % MIX_TM == 0 and (bn * s) % FFN_TM == 0
    ffn1, mixer, ffn2, final_g = _prepare(
        ffn1_norm, ffn1_w13, ffn1_w2, mix_norm, w_in, pool_w, pool_scale, sconv_w, cconv_w,
        cconv_ln_g, cconv_ln_b, sgu_ln_g, sgu_ln_b, sgu_w, sgu_b, w_up, w_out, ffn2_norm,
        ffn2_w13, ffn2_w2, final_norm)
    xf = x.reshape(bn * s, d)
    for l in range(depth):
        xf = _ffn(xf, l, *ffn1)
        xf = _mixer(xf, bn, s, l, mixer)
        xf = _ffn(xf, l, *ffn2, final_g=final_g if l == depth - 1 else None)
    return xf.reshape(bn, s, d)
```

```python
import functools

import jax
import jax.numpy as jnp
from jax import lax
from jax.experimental import pallas as pl
from jax.experimental.pallas import tpu as pltpu

D_MODEL = 1024
D_FF = 2816
BRANCH_W = 512
N_BRANCH = 4
POOL_WINDOWS = (2, 4, 8, 16)
SCONV_K = 3
CCONV_K = 31
SGU_BLOCK = 128
SGU_HEADS = 4
CHUNK = 64
EPS = 1e-6

LANES = 128
N_LT = BRANCH_W // LANES

OFF_A = 0
OFF_B = 512
OFF_C = 2048
OFF_D = 3072
OFF_G = 4096

FFN_TM = 1024
FFN_SUB = 512
FFN_TF = 256
MIX_TM = 512
HALO_A = 16
HALO_B = 8
HALO_C = 32
CONV_ROWS = 32
MERGE_COLS = 256
VMEM_LIMIT = 56 * 1024 * 1024

BF16 = jnp.bfloat16
F32 = jnp.float32


def _dot(a, b):
    return jnp.dot(a, b, preferred_element_type=F32)


def _sigmoid(x):
    return 0.5 * jnp.tanh(0.5 * x) + 0.5


def _gelu_tanh(x):
    c = 0.7978845608028654
    return 0.5 * x * (1.0 + jnp.tanh(c * (x + 0.044715 * (x * x * x))))


def _rms(x, g):
    ms = jnp.mean(x * x, axis=-1, keepdims=True)
    return x * lax.rsqrt(ms + EPS) * g


def _layer_norm(x, g, b):
    mu = jnp.mean(x, axis=-1, keepdims=True)
    xc = x - mu
    var = jnp.mean(xc * xc, axis=-1, keepdims=True)
    return xc * lax.rsqrt(var + EPS) * g + b


def _layer_spec(arr, layer):
    nd = arr.ndim
    return pl.BlockSpec((None,) + arr.shape[1:], lambda *_: (layer,) + (0,) * (nd - 1),
                        pipeline_mode=pl.Buffered(1))


def _ffn_kernel(x_ref, g_ref, w13_ref, w2_ref, *rest, final):
    if final:
        fg_ref, o_ref, h_ref, act_ref = rest
    else:
        o_ref, h_ref, act_ref = rest
    def up_chunk(c, rows):
        lo = c * FFN_TF
        h = h_ref[rows, :]
        a = _dot(h, w13_ref[:, lo:lo + FFN_TF])
        b = _dot(h, w13_ref[:, D_FF + lo:D_FF + lo + FFN_TF])
        act_ref[rows, lo:lo + FFN_TF] = (a * _sigmoid(a) * b).astype(BF16)

    for s0 in range(0, FFN_TM, FFN_SUB):
        sub = slice(s0, s0 + FFN_SUB)
        half = FFN_SUB // 2
        for r0 in (s0, s0 + half):
            rows = slice(r0, r0 + half)
            h_ref[rows, :] = _rms(x_ref[rows, :], g_ref[...]).astype(BF16)
            up_chunk(0, rows)
        for c in range(1, D_FF // FFN_TF):
            up_chunk(c, sub)
        y = _dot(act_ref[sub, :], w2_ref[...])
        out = x_ref[sub, :] + 0.5 * y
        if final:
            out = _rms(out, fg_ref[...])
        o_ref[sub, :] = out


def _ffn(x, layer, g, w13, w2, final_g=None):
    t = x.shape[0]
    final = final_g is not None
    args = [x, g, w13, w2]
    in_specs = [pl.BlockSpec((FFN_TM, D_MODEL), lambda i: (i, 0))]
    in_specs += [_layer_spec(a, layer) for a in args[1:]]
    if final:
        args.append(final_g)
        in_specs.append(_layer_spec(final_g, 0))
    return pl.pallas_call(
        functools.partial(_ffn_kernel, final=final),
        grid=(t // FFN_TM,),
        in_specs=in_specs,
        out_specs=pl.BlockSpec((FFN_TM, D_MODEL), lambda i: (i, 0)),
        out_shape=jax.ShapeDtypeStruct((t, D_MODEL), F32),
        scratch_shapes=[
            pltpu.VMEM((FFN_TM, D_MODEL), BF16),
            pltpu.VMEM((FFN_TM, D_FF), BF16),
        ],
        compiler_params=pltpu.CompilerParams(
            dimension_semantics=("arbitrary",), vmem_limit_bytes=VMEM_LIMIT),
        name="ffn_final" if final else "ffn",
    )(*args)


def _mixer_kernel(x_ref, g_ref, w_in_ref, pool_w_ref, pool_scale_ref, sconv_w_ref,
                  cconv_w_ref, cln_g_ref, cln_b_ref, sln_g_ref, sln_b_ref,
                  sgu_w_ref, sgu_bias_ref, w_up_ref, w_out_ref,
                  o_ref,
                  h_ref, ext_a, ext_b, ext_c, y_ref, u_ref, v_ref, m32_ref, m_ref):
    tm = MIX_TM
    s = pl.program_id(1)

    @pl.when(s == 0)
    def _():
        ext_a[:, 0:HALO_A, :] = jnp.zeros((N_LT, HALO_A, LANES), F32)
        ext_b[:, 0:HALO_B, :] = jnp.zeros((N_LT, HALO_B, LANES), F32)
        ext_c[:, 0:HALO_C, :] = jnp.zeros((N_LT, HALO_C, LANES), F32)

    def proj(lo, width, rows=slice(None)):
        return _dot(h_ref[rows, :], w_in_ref[:, lo:lo + width])

    def put_tiles(ext, halo, val, row0=0):
        n = val.shape[0]
        for j in range(N_LT):
            ext[j, halo + row0:halo + row0 + n, :] = val[:, j * LANES:(j + 1) * LANES]

    half = tm // 2
    for r0 in (0, half):
        rows = slice(r0, r0 + half)
        h_ref[rows, :] = _rms(x_ref[rows, :], g_ref[...]).astype(BF16)

    for r0 in (0, half):
        rows = slice(r0, r0 + half)
        glu = proj(OFF_C, BRANCH_W, rows) * _sigmoid(proj(OFF_C + BRANCH_W, BRANCH_W, rows))
        put_tiles(ext_c, HALO_C, glu, r0)

    def conv_chunk(r):
        r0 = r * CONV_ROWS
        tiles = []
        for j in range(N_LT):
            acc = None
            for k in range(CCONV_K):
                off = r0 + HALO_C - (CCONV_K - 1) + k
                term = cconv_w_ref[k:k + 1, j * LANES:(j + 1) * LANES] * ext_c[j, off:off + CONV_ROWS, :]
                acc = term if acc is None else acc + term
            tiles.append(acc)
        yc = _layer_norm(jnp.concatenate(tiles, axis=1), cln_g_ref[...], cln_b_ref[...])
        y_ref[2, r0:r0 + CONV_ROWS, :] = (yc * _sigmoid(yc)).astype(BF16)

    def task_pool():
        put_tiles(ext_a, HALO_A, proj(OFF_A, BRANCH_W))
        t_idx = s * tm + lax.broadcasted_iota(jnp.int32, (tm, 1), 0)
        for gi, win in enumerate(POOL_WINDOWS):
            cols = slice(gi * LANES, (gi + 1) * LANES)
            tok = ext_a[gi, HALO_A:HALO_A + tm, :]
            acc = tok
            for j in range(1, win):
                acc = acc + ext_a[gi, HALO_A - j:HALO_A - j + tm, :]
            inv_cnt = 1.0 / jnp.minimum(t_idx + 1, win).astype(F32)
            d = acc * inv_cnt - tok
            ya = _dot(d.astype(BF16), pool_w_ref[gi]) * pool_scale_ref[:, cols]
            y_ref[0, :, cols] = ya.astype(BF16)

    def task_sconv_in():
        put_tiles(ext_b, HALO_B, proj(OFF_B, BRANCH_W) * proj(OFF_B + 2 * BRANCH_W, BRANCH_W))

    def task_sconv_out():
        bg = proj(OFF_B + BRANCH_W, BRANCH_W)
        for j in range(N_LT):
            cols = slice(j * LANES, (j + 1) * LANES)
            conv = None
            for k in range(SCONV_K):
                off = HALO_B - (SCONV_K - 1) + k
                term = sconv_w_ref[k:k + 1, cols] * ext_b[j, off:off + tm, :]
                conv = term if conv is None else conv + term
            y_ref[1, :, cols] = (bg[:, cols] * conv).astype(BF16)

    def task_sgu_u():
        u_ref[...] = _gelu_tanh(proj(OFF_D, BRANCH_W))

    def task_sgu_v():
        v = _gelu_tanh(proj(OFF_D + BRANCH_W, BRANCH_W))
        v_ref[...] = _layer_norm(v, sln_g_ref[...], sln_b_ref[...]).astype(BF16)

    def sgu_mix(hd):
        row = lax.broadcasted_iota(jnp.int32, (SGU_BLOCK, SGU_BLOCK), 0)
        col = lax.broadcasted_iota(jnp.int32, (SGU_BLOCK, SGU_BLOCK), 1)
        mask = (col // CHUNK) <= (row // CHUNK)
        n_blk = tm // SGU_BLOCK
        hc = slice(hd * LANES, (hd + 1) * LANES)
        w = jnp.where(mask, sgu_w_ref[hd], 0.0).astype(BF16)
        rhs = jnp.concatenate(
            [v_ref[n * SGU_BLOCK:(n + 1) * SGU_BLOCK, hc] for n in range(n_blk)], axis=1)
        z = _dot(w, rhs)
        for n in range(n_blk):
            rows = slice(n * SGU_BLOCK, (n + 1) * SGU_BLOCK)
            zn = z[:, n * LANES:(n + 1) * LANES] + sgu_bias_ref[:, hc]
            y_ref[3, rows, hc] = (u_ref[rows, hc] * zn).astype(BF16)

    def gated_up(g, j):
        cols = slice(j * MERGE_COLS, (j + 1) * MERGE_COLS)
        gate = _sigmoid(proj(OFF_G + g * D_MODEL + j * MERGE_COLS, MERGE_COLS))
        return gate * _dot(y_ref[g], w_up_ref[g, :, cols])

    n_conv = tm // CONV_ROWS
    n_mc = D_MODEL // MERGE_COLS
    conv_branch = 2
    early = [g for g in range(N_BRANCH) if g != conv_branch]
    vpu_items = [functools.partial(conv_chunk, r) for r in range(n_conv)]
    head_tasks = [task_pool, task_sconv_in, task_sconv_out, task_sgu_u, task_sgu_v]
    for task in head_tasks:
        task()
        vpu_items.pop(0)()
    vpu_items = [functools.partial(sgu_mix, hd) for hd in range(SGU_HEADS)] + vpu_items
    slots = len(early) * n_mc
    for i, (g, j) in enumerate((g, j) for g in early for j in range(n_mc)):
        cols = slice(j * MERGE_COLS, (j + 1) * MERGE_COLS)
        part = gated_up(g, j)
        m32_ref[:, cols] = part if g == early[0] else m32_ref[:, cols] + part
        take = -(-len(vpu_items) // (slots - i))
        for _ in range(take):
            vpu_items.pop(0)()
    assert not vpu_items

    for j in range(N_LT):
        ext_a[j, 0:HALO_A, :] = ext_a[j, tm:tm + HALO_A, :]
        ext_b[j, 0:HALO_B, :] = ext_b[j, tm:tm + HALO_B, :]
        ext_c[j, 0:HALO_C, :] = ext_c[j, tm:tm + HALO_C, :]

    for j in range(n_mc):
        cols = slice(j * MERGE_COLS, (j + 1) * MERGE_COLS)
        m_ref[:, cols] = (m32_ref[:, cols] + gated_up(conv_branch, j)).astype(BF16)
    o_ref[...] = x_ref[...] + _dot(m_ref[...], w_out_ref[...])


def _mixer(x, batch, seq, layer, params):
    tm = MIX_TM
    n_s = seq // tm
    in_specs = [pl.BlockSpec((tm, D_MODEL), lambda b, s: (b * n_s + s, 0))]
    in_specs += [_layer_spec(a, layer) for a in params]
    return pl.pallas_call(
        _mixer_kernel,
        grid=(batch, n_s),
        in_specs=in_specs,
        out_specs=pl.BlockSpec((tm, D_MODEL), lambda b, s: (b * n_s + s, 0)),
        out_shape=jax.ShapeDtypeStruct(x.shape, F32),
        scratch_shapes=[
            pltpu.VMEM((tm, D_MODEL), BF16),
            pltpu.VMEM((N_LT, HALO_A + tm, LANES), F32),
            pltpu.VMEM((N_LT, HALO_B + tm, LANES), F32),
            pltpu.VMEM((N_LT, HALO_C + tm, LANES), F32),
            pltpu.VMEM((N_BRANCH, tm, BRANCH_W), BF16),
            pltpu.VMEM((tm, BRANCH_W), F32),
            pltpu.VMEM((tm, BRANCH_W), BF16),
            pltpu.VMEM((tm, D_MODEL), F32),
            pltpu.VMEM((tm, D_MODEL), BF16),
        ],
        compiler_params=pltpu.CompilerParams(
            dimension_semantics=("arbitrary", "arbitrary"), vmem_limit_bytes=VMEM_LIMIT),
        name="mixer",
    )(x, *params)


def _prepare(ffn1_norm, ffn1_w13, ffn1_w2, mix_norm, w_in, pool_w, pool_scale, sconv_w, cconv_w,
             cconv_ln_g, cconv_ln_b, sgu_ln_g, sgu_ln_b, sgu_w, sgu_b, w_up, w_out, ffn2_norm,
             ffn2_w13, ffn2_w2, final_norm):
    row = lambda v: v.reshape(v.shape[0], 1, v.shape[-1])
    bf = lambda w: w.astype(BF16)
    sgu_bias = jnp.repeat(jnp.swapaxes(sgu_b, 1, 2), LANES, axis=2)
    ffn1 = (row(ffn1_norm), bf(ffn1_w13), bf(ffn1_w2))
    ffn2 = (row(ffn2_norm), bf(ffn2_w13), bf(ffn2_w2))
    mixer = (row(mix_norm), bf(w_in), bf(pool_w), row(pool_scale), sconv_w, cconv_w,
             row(cconv_ln_g), row(cconv_ln_b), row(sgu_ln_g), row(sgu_ln_b), sgu_w, sgu_bias,
             bf(w_up), bf(w_out))
    return ffn1, mixer, ffn2, final_norm.reshape(1, 1, -1)


def kernel(x, ffn1_norm, ffn1_w13, ffn1_w2, mix_norm, w_in, pool_w, pool_scale, sconv_w, cconv_w,
           cconv_ln_g, cconv_ln_b, sgu_ln_g, sgu_ln_b, sgu_w, sgu_b, w_up, w_out, ffn2_norm,
           ffn2_w13, ffn2_w2, final_norm):
    bn, s, d = x.shape
    depth = w_in.shape[0]
    assert d == D_MODEL and s % MIX_TM == 0 and (bn * s) % FFN_TM == 0
    ffn1, mixer, ffn2, final_g = _prepare(
        ffn1_norm, ffn1_w13, ffn1_w2, mix_norm, w_in, pool_w, pool_scale, sconv_w, cconv_w,
        cconv_ln_g, cconv_ln_b, sgu_ln_g, sgu_ln_b, sgu_w, sgu_b, w_up, w_out, ffn2_norm,
        ffn2_w13, ffn2_w2, final_norm)
    xf = x.reshape(bn * s, d)
    for l in range(depth):
        xf = _ffn(xf, l, *ffn1)
        xf = _mixer(xf, bn, s, l, mixer)
        xf = _ffn(xf, l, *ffn2, final_g=final_g if l == depth - 1 else None)
    return xf.reshape(bn, s, d)
```

```python
import functools

import jax
import jax.numpy as jnp
from jax import lax
from jax.experimental import pallas as pl
from jax.experimental.pallas import tpu as pltpu

D_MODEL = 1024
D_FF = 2816
BRANCH_W = 512
N_BRANCH = 4
POOL_WINDOWS = (2, 4, 8, 16)
SCONV_K = 3
CCONV_K = 31
SGU_BLOCK = 128
SGU_HEADS = 4
CHUNK = 64
EPS = 1e-6
IN_COLS = 8192

LANES = 128
N_LT = BRANCH_W // LANES

OFF_A = 0
OFF_B = 512
OFF_C = 2048
OFF_D = 3072
OFF_G = 4096

FFN_TM = 1024
FFN_SUB = 512
FFN_TF = 256
FFN_PRO = 16
MIX_TM = 512
MIX_PRO = 32
HALO_A = 16
HALO_B = 8
HALO_C = 32
CONV_ROWS = 32
MERGE_COLS = 256
VMEM_LIMIT = 56 * 1024 * 1024

BF16 = jnp.bfloat16
F32 = jnp.float32


def _dot(a, b):
    return jnp.dot(a, b, preferred_element_type=F32)


def _sigmoid(x):
    return 0.5 * jnp.tanh(0.5 * x) + 0.5


def _gelu_tanh(x):
    c = 0.7978845608028654
    return 0.5 * x * (1.0 + jnp.tanh(c * (x + 0.044715 * (x * x * x))))


def _rms(x, g):
    ms = jnp.mean(x * x, axis=-1, keepdims=True)
    return x * lax.rsqrt(ms + EPS) * g


def _layer_norm(x, g, b):
    mu = jnp.mean(x, axis=-1, keepdims=True)
    xc = x - mu
    var = jnp.mean(xc * xc, axis=-1, keepdims=True)
    return xc * lax.rsqrt(var + EPS) * g + b


def _layer_spec(arr, layer):
    nd = arr.ndim
    return pl.BlockSpec((None,) + arr.shape[1:], lambda *_: (layer,) + (0,) * (nd - 1),
                        pipeline_mode=pl.Buffered(1))


def _row_chunk_spec(arr, layer, n_chunks):
    rows = arr.shape[1] // n_chunks
    assert rows * n_chunks == arr.shape[1]
    return pl.BlockSpec((None, rows, arr.shape[2]),
                        lambda i: (layer, jnp.minimum(i, n_chunks - 1), 0))


def _tile_spec(rows, n_pro):
    return pl.BlockSpec((rows, D_MODEL), lambda i: (jnp.maximum(i - n_pro, 0), 0))


def _stash_bf16(dst, src, step):
    n = src.shape[0]
    dst[pl.ds(pl.multiple_of(step * n, n), n), :] = src[...].astype(BF16)


def _ffn_kernel(x_ref, g_ref, w13_ref, w2_ref, *rest, final):
    if final:
        fg_ref, o_ref, w13_s, w2_s, h_ref, act_ref = rest
    else:
        o_ref, w13_s, w2_s, h_ref, act_ref = rest
    step = pl.program_id(0)

    @pl.when(step < FFN_PRO)
    def _():
        _stash_bf16(w13_s, w13_ref, step)
        _stash_bf16(w2_s, w2_ref, step)

    def up_chunk(c, rows):
        lo = c * FFN_TF
        h = h_ref[rows, :]
        a = _dot(h, w13_s[:, lo:lo + FFN_TF])
        b = _dot(h, w13_s[:, D_FF + lo:D_FF + lo + FFN_TF])
        act_ref[rows, lo:lo + FFN_TF] = (a * _sigmoid(a) * b).astype(BF16)

    @pl.when(step >= FFN_PRO)
    def _():
        for s0 in range(0, FFN_TM, FFN_SUB):
            sub = slice(s0, s0 + FFN_SUB)
            half = FFN_SUB // 2
            for r0 in (s0, s0 + half):
                rows = slice(r0, r0 + half)
                h_ref[rows, :] = _rms(x_ref[rows, :], g_ref[...]).astype(BF16)
                up_chunk(0, rows)
            for c in range(1, D_FF // FFN_TF):
                up_chunk(c, sub)
            y = _dot(act_ref[sub, :], w2_s[...])
            out = x_ref[sub, :] + 0.5 * y
            if final:
                out = _rms(out, fg_ref[...])
            o_ref[sub, :] = out


def _ffn(x, layer, g, w13, w2, final_g=None):
    t = x.shape[0]
    final = final_g is not None
    args = [x, g, w13, w2]
    in_specs = [_tile_spec(FFN_TM, FFN_PRO), _layer_spec(g, layer),
                _row_chunk_spec(w13, layer, FFN_PRO), _row_chunk_spec(w2, layer, FFN_PRO)]
    if final:
        args.append(final_g)
        in_specs.append(_layer_spec(final_g, 0))
    return pl.pallas_call(
        functools.partial(_ffn_kernel, final=final),
        grid=(FFN_PRO + t // FFN_TM,),
        in_specs=in_specs,
        out_specs=_tile_spec(FFN_TM, FFN_PRO),
        out_shape=jax.ShapeDtypeStruct((t, D_MODEL), F32),
        scratch_shapes=[
            pltpu.VMEM((D_MODEL, 2 * D_FF), BF16),
            pltpu.VMEM((D_FF, D_MODEL), BF16),
            pltpu.VMEM((FFN_TM, D_MODEL), BF16),
            pltpu.VMEM((FFN_TM, D_FF), BF16),
        ],
        compiler_params=pltpu.CompilerParams(
            dimension_semantics=("arbitrary",), vmem_limit_bytes=VMEM_LIMIT),
        name="ffn_final" if final else "ffn",
    )(*args)


def _mixer_kernel(x_ref, g_ref, w_in_ref, pool_w_ref, pool_scale_ref, sconv_w_ref,
                  cconv_w_ref, cln_g_ref, cln_b_ref, sln_g_ref, sln_b_ref,
                  sgu_w_ref, sgu_bias_ref, w_up_ref, w_out_ref,
                  o_ref,
                  w_in_s, w_up_s, w_out_s,
                  h_ref, ext_a, ext_b, ext_c, y_ref, u_ref, v_ref, m32_ref, m_ref, *, n_s):
    step = pl.program_id(0)

    @pl.when(step < MIX_PRO)
    def _():
        _stash_bf16(w_in_s, w_in_ref, step)
        _stash_bf16(w_up_s, w_up_ref, step)
        _stash_bf16(w_out_s, w_out_ref, step)

    @pl.when(step >= MIX_PRO)
    def _():
        _mixer_tile(lax.rem(step - MIX_PRO, n_s),
                    x_ref, g_ref, w_in_s, pool_w_ref, pool_scale_ref, sconv_w_ref,
                    cconv_w_ref, cln_g_ref, cln_b_ref, sln_g_ref, sln_b_ref,
                    sgu_w_ref, sgu_bias_ref, w_up_s, w_out_s, o_ref,
                    h_ref, ext_a, ext_b, ext_c, y_ref, u_ref, v_ref, m32_ref, m_ref)


def _mixer_tile(s, x_ref, g_ref, w_in_s, pool_w_ref, pool_scale_ref, sconv_w_ref,
                cconv_w_ref, cln_g_ref, cln_b_ref, sln_g_ref, sln_b_ref,
                sgu_w_ref, sgu_bias_ref, w_up_s, w_out_s, o_ref,
                h_ref, ext_a, ext_b, ext_c, y_ref, u_ref, v_ref, m32_ref, m_ref):
    tm = MIX_TM

    @pl.when(s == 0)
    def _():
        ext_a[:, 0:HALO_A, :] = jnp.zeros((N_LT, HALO_A, LANES), F32)
        ext_b[:, 0:HALO_B, :] = jnp.zeros((N_LT, HALO_B, LANES), F32)
        ext_c[:, 0:HALO_C, :] = jnp.zeros((N_LT, HALO_C, LANES), F32)

    def proj(lo, width, rows=slice(None)):
        return _dot(h_ref[rows, :], w_in_s[:, lo:lo + width])

    def put_tiles(ext, halo, val, row0=0):
        n = val.shape[0]
        for j in range(N_LT):
            ext[j, halo + row0:halo + row0 + n, :] = val[:, j * LANES:(j + 1) * LANES]

    half = tm // 2
    for r0 in (0, half):
        rows = slice(r0, r0 + half)
        h_ref[rows, :] = _rms(x_ref[rows, :], g_ref[...]).astype(BF16)

    def conv_in(r0):
        rows = slice(r0, r0 + half)
        glu = proj(OFF_C, BRANCH_W, rows) * _sigmoid(proj(OFF_C + BRANCH_W, BRANCH_W, rows))
        put_tiles(ext_c, HALO_C, glu, r0)

    def conv_chunk(r):
        r0 = r * CONV_ROWS
        tiles = []
        for j in range(N_LT):
            acc = None
            for k in range(CCONV_K):
                off = r0 + HALO_C - (CCONV_K - 1) + k
                term = cconv_w_ref[k:k + 1, j * LANES:(j + 1) * LANES] * ext_c[j, off:off + CONV_ROWS, :]
                acc = term if acc is None else acc + term
            tiles.append(acc)
        yc = _layer_norm(jnp.concatenate(tiles, axis=1), cln_g_ref[...], cln_b_ref[...])
        y_ref[2, r0:r0 + CONV_ROWS, :] = (yc * _sigmoid(yc)).astype(BF16)

    def task_pool():
        put_tiles(ext_a, HALO_A, proj(OFF_A, BRANCH_W))
        t_idx = s * tm + lax.broadcasted_iota(jnp.int32, (tm, 1), 0)
        for gi, win in enumerate(POOL_WINDOWS):
            cols = slice(gi * LANES, (gi + 1) * LANES)
            tok = ext_a[gi, HALO_A:HALO_A + tm, :]
            acc = tok
            for j in range(1, win):
                acc = acc + ext_a[gi, HALO_A - j:HALO_A - j + tm, :]
            inv_cnt = 1.0 / jnp.minimum(t_idx + 1, win).astype(F32)
            d = acc * inv_cnt - tok
            ya = _dot(d.astype(BF16), pool_w_ref[gi].astype(BF16)) * pool_scale_ref[:, cols]
            y_ref[0, :, cols] = ya.astype(BF16)

    def task_sconv_in():
        put_tiles(ext_b, HALO_B, proj(OFF_B, BRANCH_W) * proj(OFF_B + 2 * BRANCH_W, BRANCH_W))

    def task_sconv_out():
        bg = proj(OFF_B + BRANCH_W, BRANCH_W)
        for j in range(N_LT):
            cols = slice(j * LANES, (j + 1) * LANES)
            conv = None
            for k in range(SCONV_K):
                off = HALO_B - (SCONV_K - 1) + k
                term = sconv_w_ref[k:k + 1, cols] * ext_b[j, off:off + tm, :]
                conv = term if conv is None else conv + term
            y_ref[1, :, cols] = (bg[:, cols] * conv).astype(BF16)

    def task_sgu_u():
        u_ref[...] = _gelu_tanh(proj(OFF_D, BRANCH_W))

    def task_sgu_v():
        v = _gelu_tanh(proj(OFF_D + BRANCH_W, BRANCH_W))
        v_ref[...] = _layer_norm(v, sln_g_ref[...], sln_b_ref[...]).astype(BF16)

    def sgu_mix(hd):
        row = lax.broadcasted_iota(jnp.int32, (SGU_BLOCK, SGU_BLOCK), 0)
        col = lax.broadcasted_iota(jnp.int32, (SGU_BLOCK, SGU_BLOCK), 1)
        mask = (col // CHUNK) <= (row // CHUNK)
        n_blk = tm // SGU_BLOCK
        hc = slice(hd * LANES, (hd + 1) * LANES)
        w = jnp.where(mask, sgu_w_ref[hd], 0.0).astype(BF16)
        rhs = jnp.concatenate(
            [v_ref[n * SGU_BLOCK:(n + 1) * SGU_BLOCK, hc] for n in range(n_blk)], axis=1)
        z = _dot(w, rhs)
        for n in range(n_blk):
            rows = slice(n * SGU_BLOCK, (n + 1) * SGU_BLOCK)
            zn = z[:, n * LANES:(n + 1) * LANES] + sgu_bias_ref[:, hc]
            y_ref[3, rows, hc] = (u_ref[rows, hc] * zn).astype(BF16)

    def gated_up(g, j):
        cols = slice(j * MERGE_COLS, (j + 1) * MERGE_COLS)
        gate = _sigmoid(proj(OFF_G + g * D_MODEL + j * MERGE_COLS, MERGE_COLS))
        return gate * _dot(y_ref[g], w_up_s[g * BRANCH_W:(g + 1) * BRANCH_W, cols])

    n_conv = tm // CONV_ROWS
    n_mc = D_MODEL // MERGE_COLS
    conv_branch = 2
    early = [g for g in range(N_BRANCH) if g != conv_branch]
    vpu_items = [functools.partial(conv_chunk, r) for r in range(n_conv)]
    conv_in(0)
    conv_in(half)
    head_tasks = [task_pool, task_sconv_in, task_sconv_out, task_sgu_u, task_sgu_v]
    for task in head_tasks:
        task()
        vpu_items.pop(0)()
    vpu_items = [functools.partial(sgu_mix, hd) for hd in range(SGU_HEADS)] + vpu_items
    slots = len(early) * n_mc
    for i, (g, j) in enumerate((g, j) for g in early for j in range(n_mc)):
        cols = slice(j * MERGE_COLS, (j + 1) * MERGE_COLS)
        part = gated_up(g, j)
        m32_ref[:, cols] = part if g == early[0] else m32_ref[:, cols] + part
        take = -(-len(vpu_items) // (slots - i))
        for _ in range(take):
            vpu_items.pop(0)()
    assert not vpu_items

    for j in range(N_LT):
        ext_a[j, 0:HALO_A, :] = ext_a[j, tm:tm + HALO_A, :]
        ext_b[j, 0:HALO_B, :] = ext_b[j, tm:tm + HALO_B, :]
        ext_c[j, 0:HALO_C, :] = ext_c[j, tm:tm + HALO_C, :]

    for j in range(n_mc):
        cols = slice(j * MERGE_COLS, (j + 1) * MERGE_COLS)
        m_ref[:, cols] = (m32_ref[:, cols] + gated_up(conv_branch, j)).astype(BF16)
    o_ref[...] = x_ref[...] + _dot(m_ref[...], w_out_s[...])


def _mixer(x, seq, layer, params):
    tm = MIX_TM
    n_tiles = x.shape[0] // tm
    streamed = (1, 12, 13)
    in_specs = [_tile_spec(tm, MIX_PRO)]
    in_specs += [_row_chunk_spec(a, layer, MIX_PRO) if i in streamed else _layer_spec(a, layer)
                 for i, a in enumerate(params)]
    return pl.pallas_call(
        functools.partial(_mixer_kernel, n_s=seq // tm),
        grid=(MIX_PRO + n_tiles,),
        in_specs=in_specs,
        out_specs=_tile_spec(tm, MIX_PRO),
        out_shape=jax.ShapeDtypeStruct(x.shape, F32),
        scratch_shapes=[
            pltpu.VMEM((D_MODEL, IN_COLS), BF16),
            pltpu.VMEM((N_BRANCH * BRANCH_W, D_MODEL), BF16),
            pltpu.VMEM((D_MODEL, D_MODEL), BF16),
            pltpu.VMEM((tm, D_MODEL), BF16),
            pltpu.VMEM((N_LT, HALO_A + tm, LANES), F32),
            pltpu.VMEM((N_LT, HALO_B + tm, LANES), F32),
            pltpu.VMEM((N_LT, HALO_C + tm, LANES), F32),
            pltpu.VMEM((N_BRANCH, tm, BRANCH_W), BF16),
            pltpu.VMEM((tm, BRANCH_W), F32),
            pltpu.VMEM((tm, BRANCH_W), BF16),
            pltpu.VMEM((tm, D_MODEL), F32),
            pltpu.VMEM((tm, D_MODEL), BF16),
        ],
        compiler_params=pltpu.CompilerParams(
            dimension_semantics=("arbitrary",), vmem_limit_bytes=VMEM_LIMIT),
        name="mixer",
    )(x, *params)


def _prepare(ffn1_norm, ffn1_w13, ffn1_w2, mix_norm, w_in, pool_w, pool_scale, sconv_w, cconv_w,
             cconv_ln_g, cconv_ln_b, sgu_ln_g, sgu_ln_b, sgu_w, sgu_b, w_up, w_out, ffn2_norm,
             ffn2_w13, ffn2_w2, final_norm):
    row = lambda v: v.reshape(v.shape[0], 1, v.shape[-1])
    sgu_bias = jnp.repeat(jnp.swapaxes(sgu_b, 1, 2), LANES, axis=2)
    ffn1 = (row(ffn1_norm), ffn1_w13, ffn1_w2)
    ffn2 = (row(ffn2_norm), ffn2_w13, ffn2_w2)
    mixer = (row(mix_norm), w_in, pool_w, row(pool_scale), sconv_w, cconv_w,
             row(cconv_ln_g), row(cconv_ln_b), row(sgu_ln_g), row(sgu_ln_b), sgu_w, sgu_bias,
             w_up.reshape(w_up.shape[0], N_BRANCH * BRANCH_W, D_MODEL), w_out)
    return ffn1, mixer, ffn2, final_norm.reshape(1, 1, -1)


def kernel(x, ffn1_norm, ffn1_w13, ffn1_w2, mix_norm, w_in, pool_w, pool_scale, sconv_w, cconv_w,
           cconv_ln_g, cconv_ln_b, sgu_ln_g, sgu_ln_b, sgu_w, sgu_b, w_up, w_out, ffn2_norm,
           ffn2_w13, ffn2_w2, final_norm):
    bn, s, d = x.shape
    depth = w_in.shape[0]
    assert d == D_MODEL and s % MIX_TM == 0 and (bn * s) % FFN_TM == 0
    ffn1, mixer, ffn2, final_g = _prepare(
        ffn1_norm, ffn1_w13, ffn1_w2, mix_norm, w_in, pool_w, pool_scale, sconv_w, cconv_w,
        cconv_ln_g, cconv_ln_b, sgu_ln_g, sgu_ln_b, sgu_w, sgu_b, w_up, w_out, ffn2_norm,
        ffn2_w13, ffn2_w2, final_norm)
    xf = x.reshape(bn * s, d)
    for l in range(depth):
        xf = _ffn(xf, l, *ffn1)
        xf = _mixer(xf, s, l, mixer)
        xf = _ffn(xf, l, *ffn2, final_g=final_g if l == depth - 1 else None)
    return xf.reshape(bn, s, d)
```

```python
import functools

import jax
import jax.numpy as jnp
from jax import lax
from jax.experimental import pallas as pl
from jax.experimental.pallas import tpu as pltpu

D_MODEL = 1024
D_FF = 2816
BRANCH_W = 512
N_BRANCH = 4
POOL_WINDOWS = (2, 4, 8, 16)
SCONV_K = 3
CCONV_K = 31
SGU_BLOCK = 128
SGU_HEADS = 4
CHUNK = 64
EPS = 1e-6
IN_COLS = 8192

LANES = 128
N_LT = BRANCH_W // LANES

OFF_A = 0
OFF_B = 512
OFF_C = 2048
OFF_D = 3072
OFF_G = 4096

FFN_TM = 1024
FFN_SUB = 512
FFN_TF = 256
FFN_PRO = 8
MIX_TM = 512
MIX_PRO = 16
HALO_A = 16
HALO_B = 8
HALO_C = 32
CONV_ROWS = 32
MERGE_COLS = 256
VMEM_LIMIT = 56 * 1024 * 1024

BF16 = jnp.bfloat16
F32 = jnp.float32


def _dot(a, b):
    return jnp.dot(a, b, preferred_element_type=F32)


def _sigmoid(x):
    return 0.5 * jnp.tanh(0.5 * x) + 0.5


def _gelu_tanh(x):
    c = 0.7978845608028654
    return 0.5 * x * (1.0 + jnp.tanh(c * (x + 0.044715 * (x * x * x))))


def _rms(x, g):
    ms = jnp.mean(x * x, axis=-1, keepdims=True)
    return x * lax.rsqrt(ms + EPS) * g


def _layer_norm(x, g, b):
    mu = jnp.mean(x, axis=-1, keepdims=True)
    xc = x - mu
    var = jnp.mean(xc * xc, axis=-1, keepdims=True)
    return xc * lax.rsqrt(var + EPS) * g + b


def _layer_spec(arr, layer):
    nd = arr.ndim
    return pl.BlockSpec((None,) + arr.shape[1:], lambda *_: (layer,) + (0,) * (nd - 1),
                        pipeline_mode=pl.Buffered(1))


def _row_chunk_spec(arr, layer, n_chunks):
    rows = arr.shape[1] // n_chunks
    assert rows * n_chunks == arr.shape[1]
    return pl.BlockSpec((None, rows, arr.shape[2]),
                        lambda i: (layer, jnp.minimum(i, n_chunks - 1), 0))


def _tile_spec(rows, n_pro):
    return pl.BlockSpec((rows, D_MODEL), lambda i: (jnp.maximum(i - n_pro, 0), 0))


def _stash_bf16(dst, src, step):
    n = src.shape[0]
    dst[pl.ds(pl.multiple_of(step * n, n), n), :] = src[...].astype(BF16)


def _ffn_kernel(x_ref, g_ref, w13_ref, w2_ref, *rest, final):
    if final:
        fg_ref, o_ref, w13_s, w2_s, h_ref, act_ref = rest
    else:
        o_ref, w13_s, w2_s, h_ref, act_ref = rest
    step = pl.program_id(0)

    @pl.when(step < FFN_PRO)
    def _():
        _stash_bf16(w13_s, w13_ref, step)
        _stash_bf16(w2_s, w2_ref, step)

    def up_chunk(c, rows):
        lo = c * FFN_TF
        h = h_ref[rows, :]
        a = _dot(h, w13_s[:, lo:lo + FFN_TF])
        b = _dot(h, w13_s[:, D_FF + lo:D_FF + lo + FFN_TF])
        act_ref[rows, lo:lo + FFN_TF] = (a * _sigmoid(a) * b).astype(BF16)

    @pl.when(step >= FFN_PRO)
    def _():
        for s0 in range(0, FFN_TM, FFN_SUB):
            sub = slice(s0, s0 + FFN_SUB)
            half = FFN_SUB // 2
            for r0 in (s0, s0 + half):
                rows = slice(r0, r0 + half)
                h_ref[rows, :] = _rms(x_ref[rows, :], g_ref[...]).astype(BF16)
                up_chunk(0, rows)
            for c in range(1, D_FF // FFN_TF):
                up_chunk(c, sub)
            y = _dot(act_ref[sub, :], w2_s[...])
            out = x_ref[sub, :] + 0.5 * y
            if final:
                out = _rms(out, fg_ref[...])
            o_ref[sub, :] = out


def _ffn(x, layer, g, w13, w2, final_g=None):
    t = x.shape[0]
    final = final_g is not None
    args = [x, g, w13, w2]
    in_specs = [_tile_spec(FFN_TM, FFN_PRO), _layer_spec(g, layer),
                _row_chunk_spec(w13, layer, FFN_PRO), _row_chunk_spec(w2, layer, FFN_PRO)]
    if final:
        args.append(final_g)
        in_specs.append(_layer_spec(final_g, 0))
    return pl.pallas_call(
        functools.partial(_ffn_kernel, final=final),
        grid=(FFN_PRO + t // FFN_TM,),
        in_specs=in_specs,
        out_specs=_tile_spec(FFN_TM, FFN_PRO),
        out_shape=jax.ShapeDtypeStruct((t, D_MODEL), F32),
        scratch_shapes=[
            pltpu.VMEM((D_MODEL, 2 * D_FF), BF16),
            pltpu.VMEM((D_FF, D_MODEL), BF16),
            pltpu.VMEM((FFN_TM, D_MODEL), BF16),
            pltpu.VMEM((FFN_TM, D_FF), BF16),
        ],
        compiler_params=pltpu.CompilerParams(
            dimension_semantics=("arbitrary",), vmem_limit_bytes=VMEM_LIMIT),
        name="ffn_final" if final else "ffn",
    )(*args)


def _mixer_kernel(x_ref, g_ref, w_in_ref, pool_w_ref, pool_scale_ref, sconv_w_ref,
                  cconv_w_ref, cln_g_ref, cln_b_ref, sln_g_ref, sln_b_ref,
                  sgu_w_ref, sgu_bias_ref, w_up_ref, w_out_ref,
                  o_ref,
                  w_in_s, w_up_s, w_out_s,
                  h_ref, ext_a, ext_b, ext_c, d_ref, y0_ref, y1_ref, y2_ref, y3_ref,
                  u_ref, v_ref, m32_ref, m_ref, *, n_s):
    step = pl.program_id(0)
    y_ref = (y0_ref, y1_ref, y2_ref, y3_ref)

    @pl.when(step < MIX_PRO)
    def _():
        _stash_bf16(w_in_s, w_in_ref, step)
        _stash_bf16(w_up_s, w_up_ref, step)
        _stash_bf16(w_out_s, w_out_ref, step)

    @pl.when(step >= MIX_PRO)
    def _():
        _mixer_tile(lax.rem(step - MIX_PRO, n_s),
                    x_ref, g_ref, w_in_s, pool_w_ref, pool_scale_ref, sconv_w_ref,
                    cconv_w_ref, cln_g_ref, cln_b_ref, sln_g_ref, sln_b_ref,
                    sgu_w_ref, sgu_bias_ref, w_up_s, w_out_s, o_ref,
                    h_ref, ext_a, ext_b, ext_c, d_ref, y_ref, u_ref, v_ref, m32_ref, m_ref)


def _mixer_tile(s, x_ref, g_ref, w_in_s, pool_w_ref, pool_scale_ref, sconv_w_ref,
                cconv_w_ref, cln_g_ref, cln_b_ref, sln_g_ref, sln_b_ref,
                sgu_w_ref, sgu_bias_ref, w_up_s, w_out_s, o_ref,
                h_ref, ext_a, ext_b, ext_c, d_ref, y_ref, u_ref, v_ref, m32_ref, m_ref):
    tm = MIX_TM

    @pl.when(s == 0)
    def _():
        ext_a[:, 0:HALO_A, :] = jnp.zeros((N_LT, HALO_A, LANES), F32)
        ext_b[:, 0:HALO_B, :] = jnp.zeros((N_LT, HALO_B, LANES), F32)
        ext_c[:, 0:HALO_C, :] = jnp.zeros((N_LT, HALO_C, LANES), F32)

    def proj(lo, width, rows=slice(None)):
        return _dot(h_ref[rows, :], w_in_s[:, lo:lo + width])

    def put_tiles(ext, halo, val, row0=0):
        n = val.shape[0]
        for j in range(N_LT):
            ext[j, halo + row0:halo + row0 + n, :] = val[:, j * LANES:(j + 1) * LANES]

    half = tm // 2
    for r0 in (0, half):
        rows = slice(r0, r0 + half)
        h_ref[rows, :] = _rms(x_ref[rows, :], g_ref[...]).astype(BF16)

    def conv_in(r0):
        rows = slice(r0, r0 + half)
        glu = proj(OFF_C, BRANCH_W, rows) * _sigmoid(proj(OFF_C + BRANCH_W, BRANCH_W, rows))
        put_tiles(ext_c, HALO_C, glu, r0)

    def conv_chunk(r):
        r0 = r * CONV_ROWS
        tiles = []
        for j in range(N_LT):
            acc = None
            for k in range(CCONV_K):
                off = r0 + HALO_C - (CCONV_K - 1) + k
                term = cconv_w_ref[k:k + 1, j * LANES:(j + 1) * LANES] * ext_c[j, off:off + CONV_ROWS, :]
                acc = term if acc is None else acc + term
            tiles.append(acc)
        yc = _layer_norm(jnp.concatenate(tiles, axis=1), cln_g_ref[...], cln_b_ref[...])
        y_ref[2][r0:r0 + CONV_ROWS, :] =(yc * _sigmoid(yc)).astype(BF16)

    def task_pool_in():
        put_tiles(ext_a, HALO_A, proj(OFF_A, BRANCH_W))

    def pool_sums():
        t_idx = s * tm + lax.broadcasted_iota(jnp.int32, (tm, 1), 0)
        for gi, win in enumerate(POOL_WINDOWS):
            tok = ext_a[gi, HALO_A:HALO_A + tm, :]
            acc = tok
            for j in range(1, win):
                acc = acc + ext_a[gi, HALO_A - j:HALO_A - j + tm, :]
            inv_cnt = 1.0 / jnp.minimum(t_idx + 1, win).astype(F32)
            d_ref[:, gi * LANES:(gi + 1) * LANES] = (acc * inv_cnt - tok).astype(BF16)

    def task_pool_out():
        for gi in range(len(POOL_WINDOWS)):
            cols = slice(gi * LANES, (gi + 1) * LANES)
            ya = _dot(d_ref[:, cols], pool_w_ref[gi].astype(BF16)) * pool_scale_ref[:, cols]
            y_ref[0][:, cols] = ya.astype(BF16)

    def task_sconv_in():
        put_tiles(ext_b, HALO_B, proj(OFF_B, BRANCH_W) * proj(OFF_B + 2 * BRANCH_W, BRANCH_W))

    def task_sconv_out():
        bg = proj(OFF_B + BRANCH_W, BRANCH_W)
        for j in range(N_LT):
            cols = slice(j * LANES, (j + 1) * LANES)
            conv = None
            for k in range(SCONV_K):
                off = HALO_B - (SCONV_K - 1) + k
                term = sconv_w_ref[k:k + 1, cols] * ext_b[j, off:off + tm, :]
                conv = term if conv is None else conv + term
            y_ref[1][:, cols] = (bg[:, cols] * conv).astype(BF16)

    def task_sgu_u():
        u_ref[...] = _gelu_tanh(proj(OFF_D, BRANCH_W))

    def task_sgu_v():
        v = _gelu_tanh(proj(OFF_D + BRANCH_W, BRANCH_W))
        v_ref[...] = _layer_norm(v, sln_g_ref[...], sln_b_ref[...]).astype(BF16)

    def sgu_mix(hd):
        row = lax.broadcasted_iota(jnp.int32, (SGU_BLOCK, SGU_BLOCK), 0)
        col = lax.broadcasted_iota(jnp.int32, (SGU_BLOCK, SGU_BLOCK), 1)
        mask = (col // CHUNK) <= (row // CHUNK)
        n_blk = tm // SGU_BLOCK
        hc = slice(hd * LANES, (hd + 1) * LANES)
        w = jnp.where(mask, sgu_w_ref[hd], 0.0).astype(BF16)
        rhs = jnp.concatenate(
            [v_ref[n * SGU_BLOCK:(n + 1) * SGU_BLOCK, hc] for n in range(n_blk)], axis=1)
        z = _dot(w, rhs)
        for n in range(n_blk):
            rows = slice(n * SGU_BLOCK, (n + 1) * SGU_BLOCK)
            zn = z[:, n * LANES:(n + 1) * LANES] + sgu_bias_ref[:, hc]
            y_ref[3][rows, hc] = (u_ref[rows, hc] * zn).astype(BF16)

    def gated_up(g, j):
        cols = slice(j * MERGE_COLS, (j + 1) * MERGE_COLS)
        gate = _sigmoid(proj(OFF_G + g * D_MODEL + j * MERGE_COLS, MERGE_COLS))
        return gate * _dot(y_ref[g][...], w_up_s[g * BRANCH_W:(g + 1) * BRANCH_W, cols])

    n_conv = tm // CONV_ROWS
    n_mc = D_MODEL // MERGE_COLS
    conv_branch = 2
    early = [g for g in range(N_BRANCH) if g != conv_branch]
    conv_items = [functools.partial(conv_chunk, r) for r in range(n_conv)]

    def convs(n):
        for _ in range(n):
            conv_items.pop(0)()

    def early_items(g):
        for j in range(n_mc):
            cols = slice(j * MERGE_COLS, (j + 1) * MERGE_COLS)
            part = gated_up(g, j)
            m32_ref[:, cols] = part if g == early[0] else m32_ref[:, cols] + part
            yield j

    conv_in(0)
    conv_in(half)
    task_pool_in()
    pool_sums()
    convs(2)
    task_sconv_in()
    convs(2)
    task_pool_out()
    task_sconv_out()
    convs(2)
    task_sgu_v()
    convs(1)
    task_sgu_u()
    convs(1)
    assert n_mc == SGU_HEADS and early == [0, 1, 3]
    for hd in early_items(0):
        sgu_mix(hd)
        convs(1)
    for _ in early_items(1):
        convs(1)
    assert not conv_items
    for _ in early_items(3):
        pass

    for j in range(N_LT):
        ext_a[j, 0:HALO_A, :] = ext_a[j, tm:tm + HALO_A, :]
        ext_b[j, 0:HALO_B, :] = ext_b[j, tm:tm + HALO_B, :]
        ext_c[j, 0:HALO_C, :] = ext_c[j, tm:tm + HALO_C, :]

    for j in range(n_mc):
        cols = slice(j * MERGE_COLS, (j + 1) * MERGE_COLS)
        m_ref[:, cols] = (m32_ref[:, cols] + gated_up(conv_branch, j)).astype(BF16)
    o_ref[...] = x_ref[...] + _dot(m_ref[...], w_out_s[...])


def _mixer(x, seq, layer, params):
    tm = MIX_TM
    n_tiles = x.shape[0] // tm
    streamed = (1, 12, 13)
    in_specs = [_tile_spec(tm, MIX_PRO)]
    in_specs += [_row_chunk_spec(a, layer, MIX_PRO) if i in streamed else _layer_spec(a, layer)
                 for i, a in enumerate(params)]
    return pl.pallas_call(
        functools.partial(_mixer_kernel, n_s=seq // tm),
        grid=(MIX_PRO + n_tiles,),
        in_specs=in_specs,
        out_specs=_tile_spec(tm, MIX_PRO),
        out_shape=jax.ShapeDtypeStruct(x.shape, F32),
        scratch_shapes=[
            pltpu.VMEM((D_MODEL, IN_COLS), BF16),
            pltpu.VMEM((N_BRANCH * BRANCH_W, D_MODEL), BF16),
            pltpu.VMEM((D_MODEL, D_MODEL), BF16),
            pltpu.VMEM((tm, D_MODEL), BF16),
            pltpu.VMEM((N_LT, HALO_A + tm, LANES), F32),
            pltpu.VMEM((N_LT, HALO_B + tm, LANES), F32),
            pltpu.VMEM((N_LT, HALO_C + tm, LANES), F32),
            pltpu.VMEM((tm, BRANCH_W), BF16),
            pltpu.VMEM((tm, BRANCH_W), BF16),
            pltpu.VMEM((tm, BRANCH_W), BF16),
            pltpu.VMEM((tm, BRANCH_W), BF16),
            pltpu.VMEM((tm, BRANCH_W), BF16),
            pltpu.VMEM((tm, BRANCH_W), F32),
            pltpu.VMEM((tm, BRANCH_W), BF16),
            pltpu.VMEM((tm, D_MODEL), F32),
            pltpu.VMEM((tm, D_MODEL), BF16),
        ],
        compiler_params=pltpu.CompilerParams(
            dimension_semantics=("arbitrary",), vmem_limit_bytes=VMEM_LIMIT),
        name="mixer",
    )(x, *params)


def _prepare(ffn1_norm, ffn1_w13, ffn1_w2, mix_norm, w_in, pool_w, pool_scale, sconv_w, cconv_w,
             cconv_ln_g, cconv_ln_b, sgu_ln_g, sgu_ln_b, sgu_w, sgu_b, w_up, w_out, ffn2_norm,
             ffn2_w13, ffn2_w2, final_norm):
    row = lambda v: v.reshape(v.shape[0], 1, v.shape[-1])
    sgu_bias = jnp.repeat(jnp.swapaxes(sgu_b, 1, 2), LANES, axis=2)
    ffn1 = (row(ffn1_norm), ffn1_w13, ffn1_w2)
    ffn2 = (row(ffn2_norm), ffn2_w13, ffn2_w2)
    mixer = (row(mix_norm), w_in, pool_w, row(pool_scale), sconv_w, cconv_w,
             row(cconv_ln_g), row(cconv_ln_b), row(sgu_ln_g), row(sgu_ln_b), sgu_w, sgu_bias,
             w_up.reshape(w_up.shape[0], N_BRANCH * BRANCH_W, D_MODEL), w_out)
    return ffn1, mixer, ffn2, final_norm.reshape(1, 1, -1)


def kernel(x, ffn1_norm, ffn1_w13, ffn1_w2, mix_norm, w_in, pool_w, pool_scale, sconv_w, cconv_w,
           cconv_ln_g, cconv_ln_b, sgu_ln_g, sgu_ln_b, sgu_w, sgu_b, w_up, w_out, ffn2_norm,
           ffn2_w13, ffn2_w2, final_norm):
    bn, s, d = x.shape
    depth = w_in.shape[0]
    assert d == D_MODEL and s % MIX_TM == 0 and (bn * s) % FFN_TM == 0
    ffn1, mixer, ffn2, final_g = _prepare(
        ffn1_norm, ffn1_w13, ffn1_w2, mix_norm, w_in, pool_w, pool_scale, sconv_w, cconv_w,
        cconv_ln_g, cconv_ln_b, sgu_ln_g, sgu_ln_b, sgu_w, sgu_b, w_up, w_out, ffn2_norm,
        ffn2_w13, ffn2_w2, final_norm)
    xf = x.reshape(bn * s, d)
    for l in range(depth):
        xf = _ffn(xf, l, *ffn1)
        xf = _mixer(xf, s, l, mixer)
        xf = _ffn(xf, l, *ffn2, final_g=final_g if l == depth - 1 else None)
    return xf.reshape(bn, s, d)
```

```python
import functools

import jax
import jax.numpy as jnp
from jax import lax
from jax.experimental import pallas as pl
from jax.experimental.pallas import tpu as pltpu

D_MODEL = 1024
D_FF = 2816
BRANCH_W = 512
N_BRANCH = 4
POOL_WINDOWS = (2, 4, 8, 16)
SCONV_K = 3
CCONV_K = 31
SGU_BLOCK = 128
SGU_HEADS = 4
CHUNK = 64
EPS = 1e-6
IN_COLS = 8192

LANES = 128
N_LT = BRANCH_W // LANES

OFF_A = 0
OFF_B = 512
OFF_C = 2048
OFF_D = 3072
OFF_G = 4096

FFN_TM = 1024
FFN_SUB = 512
FFN_TF = 256
FFN_PRO = 8
MIX_TM = 512
MIX_PRO = 16
HALO_A = 32
HALO_B = 8
HALO_C = 32
CONV_ROWS = 32
MERGE_COLS = 256
VMEM_LIMIT = 56 * 1024 * 1024

BF16 = jnp.bfloat16
F32 = jnp.float32


def _dot(a, b):
    return jnp.dot(a, b, preferred_element_type=F32)


def _sigmoid(x):
    return 0.5 * jnp.tanh(0.5 * x) + 0.5


def _gelu_tanh(x):
    c = 0.7978845608028654
    return 0.5 * x * (1.0 + jnp.tanh(c * (x + 0.044715 * (x * x * x))))


def _rms(x, g):
    ms = jnp.mean(x * x, axis=-1, keepdims=True)
    return x * lax.rsqrt(ms + EPS) * g


def _layer_norm(x, g, b):
    mu = jnp.mean(x, axis=-1, keepdims=True)
    xc = x - mu
    var = jnp.mean(xc * xc, axis=-1, keepdims=True)
    return xc * lax.rsqrt(var + EPS) * g + b


def _layer_spec(arr, layer):
    nd = arr.ndim
    return pl.BlockSpec((None,) + arr.shape[1:], lambda *_: (layer,) + (0,) * (nd - 1),
                        pipeline_mode=pl.Buffered(1))


def _row_chunk_spec(arr, layer, n_chunks):
    rows = arr.shape[1] // n_chunks
    assert rows * n_chunks == arr.shape[1]
    return pl.BlockSpec((None, rows, arr.shape[2]),
                        lambda i: (layer, jnp.minimum(i, n_chunks - 1), 0))


def _tile_spec(rows, n_pro):
    return pl.BlockSpec((rows, D_MODEL), lambda i: (jnp.maximum(i - n_pro, 0), 0))


def _stash_bf16(dst, src, step):
    n = src.shape[0]
    dst[pl.ds(pl.multiple_of(step * n, n), n), :] = src[...].astype(BF16)


def _ffn_kernel(x_ref, g_ref, w13_ref, w2_ref, *rest, final):
    if final:
        fg_ref, o_ref, w13_s, w2_s, h_ref, act_ref = rest
    else:
        o_ref, w13_s, w2_s, h_ref, act_ref = rest
    step = pl.program_id(0)

    @pl.when(step < FFN_PRO)
    def _():
        _stash_bf16(w13_s, w13_ref, step)
        _stash_bf16(w2_s, w2_ref, step)

    def up_chunk(c, rows):
        lo = c * FFN_TF
        h = h_ref[rows, :]
        a = _dot(h, w13_s[:, lo:lo + FFN_TF])
        b = _dot(h, w13_s[:, D_FF + lo:D_FF + lo + FFN_TF])
        act_ref[rows, lo:lo + FFN_TF] = (a * _sigmoid(a) * b).astype(BF16)

    @pl.when(step >= FFN_PRO)
    def _():
        for s0 in range(0, FFN_TM, FFN_SUB):
            sub = slice(s0, s0 + FFN_SUB)
            half = FFN_SUB // 2
            for r0 in (s0, s0 + half):
                rows = slice(r0, r0 + half)
                h_ref[rows, :] = _rms(x_ref[rows, :], g_ref[...]).astype(BF16)
                up_chunk(0, rows)
            for c in range(1, D_FF // FFN_TF):
                up_chunk(c, sub)
            if final:
                for r0 in (s0, s0 + half):
                    rows = slice(r0, r0 + half)
                    out = x_ref[rows, :] + 0.5 * _dot(act_ref[rows, :], w2_s[...])
                    o_ref[rows, :] = _rms(out, fg_ref[...])
            else:
                o_ref[sub, :] = x_ref[sub, :] + 0.5 * _dot(act_ref[sub, :], w2_s[...])


def _ffn(x, layer, g, w13, w2, final_g=None):
    t = x.shape[0]
    final = final_g is not None
    args = [x, g, w13, w2]
    in_specs = [_tile_spec(FFN_TM, FFN_PRO), _layer_spec(g, layer),
                _row_chunk_spec(w13, layer, FFN_PRO), _row_chunk_spec(w2, layer, FFN_PRO)]
    if final:
        args.append(final_g)
        in_specs.append(_layer_spec(final_g, 0))
    return pl.pallas_call(
        functools.partial(_ffn_kernel, final=final),
        grid=(FFN_PRO + t // FFN_TM,),
        in_specs=in_specs,
        out_specs=_tile_spec(FFN_TM, FFN_PRO),
        out_shape=jax.ShapeDtypeStruct((t, D_MODEL), F32),
        scratch_shapes=[
            pltpu.VMEM((D_MODEL, 2 * D_FF), BF16),
            pltpu.VMEM((D_FF, D_MODEL), BF16),
            pltpu.VMEM((FFN_TM, D_MODEL), BF16),
            pltpu.VMEM((FFN_TM, D_FF), BF16),
        ],
        compiler_params=pltpu.CompilerParams(
            dimension_semantics=("arbitrary",), vmem_limit_bytes=VMEM_LIMIT),
        name="ffn_final" if final else "ffn",
    )(*args)


def _mixer_kernel(x_ref, g_ref, w_in_ref, pool_w_ref, pool_scale_ref, sconv_w_ref,
                  cconv_w_ref, cln_g_ref, cln_b_ref, sln_g_ref, sln_b_ref,
                  sgu_w_ref, sgu_bias_ref, w_up_ref, w_out_ref,
                  o_ref,
                  w_in_s, w_up_s, w_out_s,
                  h_ref, ext_a, ext_b, ext_c, pool_tmp, d_ref, y0_ref, y1_ref, y2_ref, y3_ref,
                  u_ref, v_ref, m_ref, *, n_s):
    step = pl.program_id(0)
    y_ref = (y0_ref, y1_ref, y2_ref, y3_ref)

    @pl.when(step < MIX_PRO)
    def _():
        _stash_bf16(w_in_s, w_in_ref, step)
        _stash_bf16(w_up_s, w_up_ref, step)
        _stash_bf16(w_out_s, w_out_ref, step)

    @pl.when(step >= MIX_PRO)
    def _():
        _mixer_tile(lax.rem(step - MIX_PRO, n_s),
                    x_ref, g_ref, w_in_s, pool_w_ref, pool_scale_ref, sconv_w_ref,
                    cconv_w_ref, cln_g_ref, cln_b_ref, sln_g_ref, sln_b_ref,
                    sgu_w_ref, sgu_bias_ref, w_up_s, w_out_s, o_ref,
                    h_ref, ext_a, ext_b, ext_c, pool_tmp, d_ref, y_ref, u_ref, v_ref, m_ref)


def _mixer_tile(s, x_ref, g_ref, w_in_s, pool_w_ref, pool_scale_ref, sconv_w_ref,
                cconv_w_ref, cln_g_ref, cln_b_ref, sln_g_ref, sln_b_ref,
                sgu_w_ref, sgu_bias_ref, w_up_s, w_out_s, o_ref,
                h_ref, ext_a, ext_b, ext_c, pool_tmp, d_ref, y_ref, u_ref, v_ref, m_ref):
    tm = MIX_TM

    @pl.when(s == 0)
    def _():
        ext_a[:, 0:HALO_A, :] = jnp.zeros((N_LT, HALO_A, LANES), F32)
        ext_b[:, 0:HALO_B, :] = jnp.zeros((N_LT, HALO_B, LANES), F32)
        ext_c[:, 0:HALO_C, :] = jnp.zeros((N_LT, HALO_C, LANES), F32)

    def proj(lo, width, rows=slice(None)):
        return _dot(h_ref[rows, :], w_in_s[:, lo:lo + width])

    def put_tiles(ext, halo, val, row0=0):
        n = val.shape[0]
        for j in range(N_LT):
            ext[j, halo + row0:halo + row0 + n, :] = val[:, j * LANES:(j + 1) * LANES]

    half = tm // 2
    for r0 in (0, half):
        rows = slice(r0, r0 + half)
        h_ref[rows, :] = _rms(x_ref[rows, :], g_ref[...]).astype(BF16)

    def conv_in(r0):
        rows = slice(r0, r0 + half)
        glu = proj(OFF_C, BRANCH_W, rows) * _sigmoid(proj(OFF_C + BRANCH_W, BRANCH_W, rows))
        put_tiles(ext_c, HALO_C, glu, r0)

    def conv_chunk(r):
        r0 = r * CONV_ROWS
        tiles = []
        for j in range(N_LT):
            acc = None
            for k in range(CCONV_K):
                off = r0 + HALO_C - (CCONV_K - 1) + k
                term = cconv_w_ref[k:k + 1, j * LANES:(j + 1) * LANES] * ext_c[j, off:off + CONV_ROWS, :]
                acc = term if acc is None else acc + term
            tiles.append(acc)
        yc = _layer_norm(jnp.concatenate(tiles, axis=1), cln_g_ref[...], cln_b_ref[...])
        y_ref[2][r0:r0 + CONV_ROWS, :] = (yc * _sigmoid(yc)).astype(BF16)

    def task_pool_in():
        put_tiles(ext_a, HALO_A, proj(OFF_A, BRANCH_W))

    def pool_sums():
        end = HALO_A + tm
        head = 8 * len(POOL_WINDOWS)
        t_head = s * tm + lax.broadcasted_iota(jnp.int32, (head, 1), 0)
        for gi, win in enumerate(POOL_WINDOWS):
            levels = win.bit_length() - 1
            assert win == 1 << levels and 8 * levels <= HALO_A and win // 2 <= 8
            src = ext_a.at[gi]
            for lvl in range(1, levels + 1):
                lo = HALO_A if lvl == levels else 8 * lvl
                sh = 1 << (lvl - 1)
                acc = src[lo:end, :] + src[lo - sh:end - sh, :]
                if lvl < levels:
                    dst = pool_tmp.at[lvl % 2]
                    dst[lo:end, :] = acc
                    src = dst
            tok = ext_a[gi, HALO_A:end, :]
            cols = slice(gi * LANES, (gi + 1) * LANES)
            inv_head = 1.0 / jnp.minimum(t_head + 1, win).astype(F32)
            d_ref[0:head, cols] = (acc[0:head] * inv_head - tok[0:head]).astype(BF16)
            d_ref[head:tm, cols] = (acc[head:] * (1.0 / win) - tok[head:]).astype(BF16)

    def task_pool_out():
        for gi in range(len(POOL_WINDOWS)):
            cols = slice(gi * LANES, (gi + 1) * LANES)
            ya = _dot(d_ref[:, cols], pool_w_ref[gi].astype(BF16)) * pool_scale_ref[:, cols]
            y_ref[0][:, cols] = ya.astype(BF16)

    def task_sconv_in():
        put_tiles(ext_b, HALO_B, proj(OFF_B, BRANCH_W) * proj(OFF_B + 2 * BRANCH_W, BRANCH_W))

    def task_sconv_out():
        bg = proj(OFF_B + BRANCH_W, BRANCH_W)
        for j in range(N_LT):
            cols = slice(j * LANES, (j + 1) * LANES)
            conv = None
            for k in range(SCONV_K):
                off = HALO_B - (SCONV_K - 1) + k
                term = sconv_w_ref[k:k + 1, cols] * ext_b[j, off:off + tm, :]
                conv = term if conv is None else conv + term
            y_ref[1][:, cols] = (bg[:, cols] * conv).astype(BF16)

    def task_sgu_u():
        u_ref[...] = _gelu_tanh(proj(OFF_D, BRANCH_W))

    def task_sgu_v():
        v = _gelu_tanh(proj(OFF_D + BRANCH_W, BRANCH_W))
        v_ref[...] = _layer_norm(v, sln_g_ref[...], sln_b_ref[...]).astype(BF16)

    def sgu_mix(hd):
        row = lax.broadcasted_iota(jnp.int32, (SGU_BLOCK, SGU_BLOCK), 0)
        col = lax.broadcasted_iota(jnp.int32, (SGU_BLOCK, SGU_BLOCK), 1)
        mask = (col // CHUNK) <= (row // CHUNK)
        n_blk = tm // SGU_BLOCK
        hc = slice(hd * LANES, (hd + 1) * LANES)
        w = jnp.where(mask, sgu_w_ref[hd], 0.0).astype(BF16)
        rhs = jnp.concatenate(
            [v_ref[n * SGU_BLOCK:(n + 1) * SGU_BLOCK, hc] for n in range(n_blk)], axis=1)
        z = _dot(w, rhs)
        for n in range(n_blk):
            rows = slice(n * SGU_BLOCK, (n + 1) * SGU_BLOCK)
            zn = z[:, n * LANES:(n + 1) * LANES] + sgu_bias_ref[:, hc]
            y_ref[3][rows, hc] = (u_ref[rows, hc] * zn).astype(BF16)

    def gated_up(g, j):
        cols = slice(j * MERGE_COLS, (j + 1) * MERGE_COLS)
        gate2 = 1.0 + jnp.tanh(0.5 * proj(OFF_G + g * D_MODEL + j * MERGE_COLS, MERGE_COLS))
        return gate2 * _dot(y_ref[g][...], w_up_s[g * BRANCH_W:(g + 1) * BRANCH_W, cols])

    n_conv = tm // CONV_ROWS
    n_mc = D_MODEL // MERGE_COLS
    conv_branch = 2
    early = [g for g in range(N_BRANCH) if g != conv_branch]
    conv_items = [functools.partial(conv_chunk, r) for r in range(n_conv)]

    def convs(n):
        for _ in range(min(n, len(conv_items))):
            conv_items.pop(0)()

    def early_items(g):
        for j in range(n_mc):
            cols = slice(j * MERGE_COLS, (j + 1) * MERGE_COLS)
            part = gated_up(g, j)
            o_ref[:, cols] = part if g == early[0] else o_ref[:, cols] + part
            yield j

    conv_in(0)
    conv_in(half)
    task_pool_in()
    pool_sums()
    convs(2)
    task_sconv_in()
    convs(2)
    task_pool_out()
    task_sconv_out()
    convs(2)
    task_sgu_v()
    convs(1)
    task_sgu_u()
    convs(1)
    assert n_mc == SGU_HEADS and early == [0, 1, 3]
    for hd in early_items(0):
        sgu_mix(hd)
        convs(1)
    for _ in early_items(1):
        convs(1)
    convs(len(conv_items))
    assert not conv_items
    for _ in early_items(3):
        pass

    for j in range(N_LT):
        ext_a[j, 0:HALO_A, :] = ext_a[j, tm:tm + HALO_A, :]
        ext_b[j, 0:HALO_B, :] = ext_b[j, tm:tm + HALO_B, :]
        ext_c[j, 0:HALO_C, :] = ext_c[j, tm:tm + HALO_C, :]

    for j in range(n_mc):
        cols = slice(j * MERGE_COLS, (j + 1) * MERGE_COLS)
        m_ref[:, cols] = (0.5 * (o_ref[:, cols] + gated_up(conv_branch, j))).astype(BF16)
    o_ref[...] = x_ref[...] + _dot(m_ref[...], w_out_s[...])


def _mixer(x, seq, layer, params):
    tm = MIX_TM
    n_tiles = x.shape[0] // tm
    streamed = (1, 12, 13)
    in_specs = [_tile_spec(tm, MIX_PRO)]
    in_specs += [_row_chunk_spec(a, layer, MIX_PRO) if i in streamed else _layer_spec(a, layer)
                 for i, a in enumerate(params)]
    return pl.pallas_call(
        functools.partial(_mixer_kernel, n_s=seq // tm),
        grid=(MIX_PRO + n_tiles,),
        in_specs=in_specs,
        out_specs=_tile_spec(tm, MIX_PRO),
        out_shape=jax.ShapeDtypeStruct(x.shape, F32),
        scratch_shapes=[
            pltpu.VMEM((D_MODEL, IN_COLS), BF16),
            pltpu.VMEM((N_BRANCH * BRANCH_W, D_MODEL), BF16),
            pltpu.VMEM((D_MODEL, D_MODEL), BF16),
            pltpu.VMEM((tm, D_MODEL), BF16),
            pltpu.VMEM((N_LT, HALO_A + tm, LANES), F32),
            pltpu.VMEM((N_LT, HALO_B + tm, LANES), F32),
            pltpu.VMEM((N_LT, HALO_C + tm, LANES), F32),
            pltpu.VMEM((2, HALO_A + tm, LANES), F32),
            pltpu.VMEM((tm, BRANCH_W), BF16),
            pltpu.VMEM((tm, BRANCH_W), BF16),
            pltpu.VMEM((tm, BRANCH_W), BF16),
            pltpu.VMEM((tm, BRANCH_W), BF16),
            pltpu.VMEM((tm, BRANCH_W), BF16),
            pltpu.VMEM((tm, BRANCH_W), F32),
            pltpu.VMEM((tm, BRANCH_W), BF16),
            pltpu.VMEM((tm, D_MODEL), BF16),
        ],
        compiler_params=pltpu.CompilerParams(
            dimension_semantics=("arbitrary",), vmem_limit_bytes=VMEM_LIMIT),
        name="mixer",
    )(x, *params)


def _prepare(ffn1_norm, ffn1_w13, ffn1_w2, mix_norm, w_in, pool_w, pool_scale, sconv_w, cconv_w,
             cconv_ln_g, cconv_ln_b, sgu_ln_g, sgu_ln_b, sgu_w, sgu_b, w_up, w_out, ffn2_norm,
             ffn2_w13, ffn2_w2, final_norm):
    row = lambda v: v.reshape(v.shape[0], 1, v.shape[-1])
    sgu_bias = jnp.repeat(jnp.swapaxes(sgu_b, 1, 2), LANES, axis=2)
    ffn1 = (row(ffn1_norm), ffn1_w13, ffn1_w2)
    ffn2 = (row(ffn2_norm), ffn2_w13, ffn2_w2)
    mixer = (row(mix_norm), w_in, pool_w, row(pool_scale), sconv_w, cconv_w,
             row(cconv_ln_g), row(cconv_ln_b), row(sgu_ln_g), row(sgu_ln_b), sgu_w, sgu_bias,
             w_up.reshape(w_up.shape[0], N_BRANCH * BRANCH_W, D_MODEL), w_out)
    return ffn1, mixer, ffn2, final_norm.reshape(1, 1, -1)


def kernel(x, ffn1_norm, ffn1_w13, ffn1_w2, mix_norm, w_in, pool_w, pool_scale, sconv_w, cconv_w,
           cconv_ln_g, cconv_ln_b, sgu_ln_g, sgu_ln_b, sgu_w, sgu_b, w_up, w_out, ffn2_norm,
           ffn2_w13, ffn2_w2, final_norm):
    bn, s, d = x.shape
    depth = w_in.shape[0]
    assert d == D_MODEL and s % MIX_TM == 0 and (bn * s) % FFN_TM == 0
    ffn1, mixer, ffn2, final_g = _prepare(
        ffn1_norm, ffn1_w13, ffn1_w2, mix_norm, w_in, pool_w, pool_scale, sconv_w, cconv_w,
        cconv_ln_g, cconv_ln_b, sgu_ln_g, sgu_ln_b, sgu_w, sgu_b, w_up, w_out, ffn2_norm,
        ffn2_w13, ffn2_w2, final_norm)
    xf = x.reshape(bn * s, d)
    for l in range(depth):
        xf = _ffn(xf, l, *ffn1)
        xf = _mixer(xf, s, l, mixer)
        xf = _ffn(xf, l, *ffn2, final_g=final_g if l == depth - 1 else None)
    return xf.reshape(bn, s, d)
```
